```python
import math
import jax, jax.numpy as jnp
from jax import lax
import numpy as np

D_MODEL = 1024
BATCH = 8
SEQ = 4096
DEPTH = 4

MIX_WIDTH = D_MODEL
SGU_HEADS = 4
SGU_HEAD_DIM = 128
SGU_WIDTH = SGU_HEADS * SGU_HEAD_DIM
SGU_CHUNK = 128
SGU_W_STD = 0.05
GDN_HEADS = 4
GDN_DK = 128
GDN_DV = 128
GDN_QK_WIDTH = GDN_HEADS * GDN_DK
GDN_V_WIDTH = GDN_HEADS * GDN_DV
GDN_CHUNK = 64
CONV_WIDTH = 4
CONV_CHANNELS = 2 * GDN_QK_WIDTH + GDN_V_WIDTH
D_FF = 4 * D_MODEL
N_MOD = 6
RMS_EPS = 1e-6
LN_EPS = 1e-5
IN_SIZES = (SGU_WIDTH, SGU_WIDTH, GDN_QK_WIDTH, GDN_QK_WIDTH, GDN_V_WIDTH, GDN_V_WIDTH, GDN_HEADS, GDN_HEADS)
IN_WIDTH = 2 * SGU_WIDTH + 2 * GDN_QK_WIDTH + 2 * GDN_V_WIDTH + 2 * GDN_HEADS

kernel_name = 'hybrid_sgu_gdn_adaln_trunk'


def rmsnorm(x, g):
    xf = x.astype(jnp.float32)
    y = xf * lax.rsqrt(jnp.mean(xf * xf, axis=-1, keepdims=True) + RMS_EPS)
    return (y * g.astype(jnp.float32)).astype(x.dtype)


def layernorm(x, g, b):
    xf = x.astype(jnp.float32)
    mu = jnp.mean(xf, axis=-1, keepdims=True)
    xc = xf - mu
    y = xc * lax.rsqrt(jnp.mean(xc * xc, axis=-1, keepdims=True) + LN_EPS)
    return (y * g.astype(jnp.float32) + b.astype(jnp.float32)).astype(x.dtype)


def l2norm(x):
    xf = x.astype(jnp.float32)
    return xf * lax.rsqrt(jnp.sum(xf * xf, axis=-1, keepdims=True) + RMS_EPS)


def split_cols(p, sizes):
    out, start = [], 0
    for s in sizes:
        out.append(p[..., start:start + s])
        start += s
    return out


def causal_depthwise_conv(x, w):
    k = w.shape[0]
    return lax.conv_general_dilated(x, w[:, None, :], window_strides=(1,), padding=[(k - 1, 0)],
                                    dimension_numbers=('NWC', 'WIO', 'NWC'),
                                    feature_group_count=x.shape[-1])


def sgu_mixer(u, v, ln_g, ln_b, w_s, b_s):
    bsz, t, _ = u.shape
    n = t // SGU_CHUNK
    v = layernorm(v, ln_g, ln_b)
    vc = v.reshape(bsz, n, SGU_CHUNK, SGU_HEADS, SGU_HEAD_DIM)
    causal = jnp.tril(jnp.ones((SGU_CHUNK, SGU_CHUNK), dtype=bool))
    w = jnp.where(causal[None], w_s, 0.0)
    mixed = jnp.einsum('hts,bnshd->bnthd', w, vc) + b_s.T[None, None, :, :, None]
    return u * mixed.reshape(bsz, t, SGU_WIDTH)


def gated_delta_rule(q, k, v, g, beta):
    bsz, t, h, dk = q.shape
    dv = v.shape[-1]
    c = GDN_CHUNK
    n = t // c
    q = q.reshape(bsz, n, c, h, dk).transpose(0, 3, 1, 2, 4)
    k = k.reshape(bsz, n, c, h, dk).transpose(0, 3, 1, 2, 4)
    v = v.reshape(bsz, n, c, h, dv).transpose(0, 3, 1, 2, 4)
    g = g.reshape(bsz, n, c, h).transpose(0, 3, 1, 2)
    beta = beta.reshape(bsz, n, c, h).transpose(0, 3, 1, 2)
    g_cum = jnp.cumsum(g, axis=-1)
    idx = jnp.arange(c)
    incl = idx[:, None] >= idx[None, :]
    strict = idx[:, None] > idx[None, :]
    diff = g_cum[..., :, None] - g_cum[..., None, :]
    decay = jnp.where(incl, jnp.exp(jnp.where(incl, diff, 0.0)), 0.0)
    kk = jnp.einsum('bhntd,bhnsd->bhnts', k, k)
    m = jnp.where(strict, beta[..., :, None] * kk * decay, 0.0)
    a_mat = jnp.eye(c, dtype=jnp.float32) + m
    gamma = jnp.exp(g_cum)
    rhs = jnp.concatenate([beta[..., None] * v, (beta * gamma)[..., None] * k], axis=-1)
    sol = lax.linalg.triangular_solve(a_mat, rhs, left_side=True, lower=True, unit_diagonal=True)
    u_new = sol[..., :dv]
    w_k = sol[..., dv:]
    qk = jnp.einsum('bhntd,bhnsd->bhnts', q, k) * decay
    q_dec = q * gamma[..., None]
    k_dec = k * jnp.exp(g_cum[..., -1:] - g_cum)[..., None]
    gamma_last = gamma[..., -1]

    def step(s, inp):
        q_d, k_d, u_c, wk_c, a_c, gl = inp
        w = u_c - jnp.einsum('bhck,bhkv->bhcv', wk_c, s)
        o = jnp.einsum('bhck,bhkv->bhcv', q_d, s) + jnp.einsum('bhts,bhsv->bhtv', a_c, w)
        s = gl[..., None, None] * s + jnp.einsum('bhck,bhcv->bhkv', k_d, w)
        return s, o

    xs = tuple(jnp.moveaxis(a, 2, 0) for a in (q_dec, k_dec, u_new, w_k, qk, gamma_last))
    s0 = jnp.zeros((bsz, h, dk, dv), jnp.float32)
    _, o = lax.scan(step, s0, xs)
    return o.transpose(1, 0, 3, 2, 4).reshape(bsz, t, h, dv)


def gdn_mixer(q, k, v, z, b_raw, a_raw, conv_w, a_log, dt_bias, norm_g):
    bsz, t, _ = q.shape
    dtype = q.dtype
    qkv = jax.nn.silu(causal_depthwise_conv(jnp.concatenate([q, k, v], axis=-1), conv_w))
    q, k, v = split_cols(qkv, (GDN_QK_WIDTH, GDN_QK_WIDTH, GDN_V_WIDTH))
    q = l2norm(q.reshape(bsz, t, GDN_HEADS, GDN_DK)) * (GDN_DK ** -0.5)
    k = l2norm(k.reshape(bsz, t, GDN_HEADS, GDN_DK))
    v = v.reshape(bsz, t, GDN_HEADS, GDN_DV).astype(jnp.float32)
    beta = jax.nn.sigmoid(b_raw.astype(jnp.float32))
    g = -jnp.exp(a_log.astype(jnp.float32)) * jax.nn.softplus(a_raw.astype(jnp.float32) + dt_bias.astype(jnp.float32))
    o = gated_delta_rule(q, k, v, g, beta)
    o = rmsnorm(o, norm_g) * jax.nn.silu(z.reshape(bsz, t, GDN_HEADS, GDN_DV).astype(jnp.float32))
    return o.reshape(bsz, t, GDN_V_WIDTH).astype(dtype)


def setup_inputs(seed: int = 0) -> dict:
    key = jax.random.key(seed)
    ks = jax.random.split(key, 20)
    d = D_MODEL

    def nrm(k, shape, std):
        return jax.random.normal(k, shape, jnp.float32) * std

    dt = jnp.exp(jax.random.uniform(ks[12], (DEPTH, GDN_HEADS), jnp.float32,
                                    minval=math.log(1e-3), maxval=math.log(1e-1)))
    return {
        'x': nrm(ks[0], (BATCH, SEQ, d), 1.0),
        'c': nrm(ks[1], (BATCH, d), 1.0),
        'w_ada': nrm(ks[2], (DEPTH, d, N_MOD * d), 0.5 * d ** -0.5),
        'b_ada': nrm(ks[3], (DEPTH, N_MOD * d), 0.01),
        'norm1_g': 1.0 + nrm(ks[4], (DEPTH, d), 0.02),
        'w_in': nrm(ks[5], (DEPTH, d, IN_WIDTH), d ** -0.5),
        'sgu_ln_g': 1.0 + nrm(ks[6], (DEPTH, SGU_WIDTH), 0.02),
        'sgu_ln_b': nrm(ks[7], (DEPTH, SGU_WIDTH), 0.02),
        'sgu_w': nrm(ks[8], (DEPTH, SGU_HEADS, SGU_CHUNK, SGU_CHUNK), SGU_W_STD),
        'sgu_b': 1.0 + nrm(ks[9], (DEPTH, SGU_HEADS, SGU_CHUNK), 0.1),
        'conv_w': nrm(ks[10], (DEPTH, CONV_WIDTH, CONV_CHANNELS), CONV_WIDTH ** -0.5),
        'a_log': jnp.log(jax.random.uniform(ks[11], (DEPTH, GDN_HEADS), jnp.float32, minval=1.0, maxval=16.0)),
        'dt_bias': dt + jnp.log(-jnp.expm1(-dt)),
        'gdn_norm_g': 1.0 + nrm(ks[13], (DEPTH, GDN_DV), 0.02),
        'w_out': nrm(ks[14], (DEPTH, MIX_WIDTH, d), MIX_WIDTH ** -0.5),
        'norm2_g': 1.0 + nrm(ks[15], (DEPTH, d), 0.02),
        'w_ff1': nrm(ks[16], (DEPTH, d, D_FF), d ** -0.5),
        'w_ff2': nrm(ks[17], (DEPTH, D_FF, d), D_FF ** -0.5),
        'final_g': 1.0 + nrm(ks[18], (d,), 0.02),
    }


def reference(x, c, w_ada, b_ada, norm1_g, w_in, sgu_ln_g, sgu_ln_b, sgu_w, sgu_b, conv_w,
              a_log, dt_bias, gdn_norm_g, w_out, norm2_g, w_ff1, w_ff2, final_g):
    c_act = jax.nn.silu(c)
    for l in range(DEPTH):
        mod = (c_act @ w_ada[l] + b_ada[l])[:, None, :]
        sh1, sc1, g1, sh2, sc2, g2 = jnp.split(mod, N_MOD, axis=-1)
        h = rmsnorm(x, norm1_g[l]) * (1.0 + sc1) + sh1
        p = h @ w_in[l]
        u, vs, q, k, v, z, b_raw, a_raw = split_cols(p, IN_SIZES)
        y_sgu = sgu_mixer(jax.nn.gelu(u), jax.nn.gelu(vs), sgu_ln_g[l], sgu_ln_b[l], sgu_w[l], sgu_b[l])
        y_gdn = gdn_mixer(q, k, v, z, b_raw, a_raw, conv_w[l], a_log[l], dt_bias[l], gdn_norm_g[l])
        mix = jnp.concatenate([y_sgu, y_gdn], axis=-1)
        x = x + g1 * (mix @ w_out[l])
        h = rmsnorm(x, norm2_g[l]) * (1.0 + sc2) + sh2
        x = x + g2 * (jnp.square(jax.nn.relu(h @ w_ff1[l])) @ w_ff2[l])
    return rmsnorm(x, final_g)
```

```python
import functools

import jax
import jax.numpy as jnp
from jax import lax
from jax.experimental import pallas as pl
from jax.experimental.pallas import tpu as pltpu

F32 = jnp.float32
BF16 = jnp.bfloat16

RMS_EPS = 1e-6
LN_EPS = 1e-5
N_MOD = 6
SGU_HEADS = 4
SGU_CHUNK = 128
GDN_HEADS = 4
GDN_DK = 128
GDN_DV = 128
GDN_CHUNK = 64
CONV_WIDTH = 4

LANES = 128
SUBLANES = 8
VMEM_LIMIT = 56 * 1024 * 1024


def _bdot(a, b):
    return jnp.dot(a.astype(BF16), b.astype(BF16), preferred_element_type=F32)


def _bdot_nt(a, b):
    return lax.dot_general(a.astype(BF16), b.astype(BF16), (((1,), (1,)), ((), ())),
                           preferred_element_type=F32)


def _bdot_tn(a, b):
    return lax.dot_general(a.astype(BF16), b.astype(BF16), (((0,), (0,)), ((), ())),
                           preferred_element_type=F32)


def _split3(a):
    hi = a.astype(BF16)
    r1 = a - hi.astype(F32)
    mid = r1.astype(BF16)
    lo = (r1 - mid.astype(F32)).astype(BF16)
    return hi, mid, lo


def _dot3(a, b):
    a_hi = a.astype(BF16)
    a_lo = (a - a_hi.astype(F32)).astype(BF16)
    b_hi = b.astype(BF16)
    b_lo = (b - b_hi.astype(F32)).astype(BF16)
    d = functools.partial(jnp.dot, preferred_element_type=F32)
    return d(a_hi, b_hi) + (d(a_hi, b_lo) + d(a_lo, b_hi))


def _ada_kernel(c_ref, w_ref, b_ref, o_ref):
    c = c_ref[...]
    c_act = c * jax.nn.sigmoid(c)
    o_ref[0] = _bdot(c_act, w_ref[0]) + b_ref[0]


def _ada_call(c, w_ada, b_ada):
    depth, d, n = w_ada.shape
    bsz = c.shape[0]
    tn = 1536
    assert n % tn == 0
    return pl.pallas_call(
        _ada_kernel,
        grid=(depth, n // tn),
        in_specs=[
            pl.BlockSpec((bsz, d), lambda l, j: (0, 0)),
            pl.BlockSpec((1, d, tn), lambda l, j: (l, 0, j)),
            pl.BlockSpec((1, 1, tn), lambda l, j: (l, 0, j)),
        ],
        out_specs=pl.BlockSpec((1, bsz, tn), lambda l, j: (l, 0, j)),
        out_shape=jax.ShapeDtypeStruct((depth, bsz, n), F32),
        compiler_params=pltpu.CompilerParams(
            dimension_semantics=("arbitrary", "arbitrary"), vmem_limit_bytes=VMEM_LIMIT),
        name="ada_mod",
    )(c, w_ada, b_ada.reshape(depth, 1, n))


def _inproj_kernel(x_ref, sh_ref, sc_ref, g_ref, w_ref, lng_ref, lnb_ref, sw_ref, sb_ref,
                   ysgu_ref, qkv_ref, z_ref, ba_ref, *, sgu_w, qk_w, v_w):
    x = x_ref[0]
    tm = x.shape[0]
    ms = jnp.mean(x * x, axis=-1, keepdims=True)
    h = x * lax.rsqrt(ms + RMS_EPS) * g_ref[...]
    h = h * (1.0 + sc_ref[0]) + sh_ref[0]
    p = _bdot(h, w_ref[...])

    u = jax.nn.gelu(p[:, :sgu_w])
    vs = jax.nn.gelu(p[:, sgu_w:2 * sgu_w])
    mu = jnp.mean(vs, axis=-1, keepdims=True)
    vc = vs - mu
    var = jnp.mean(vc * vc, axis=-1, keepdims=True)
    vln = vc * lax.rsqrt(var + LN_EPS) * lng_ref[...] + lnb_ref[...]

    hd = sgu_w // SGU_HEADS
    row = lax.broadcasted_iota(jnp.int32, (SGU_CHUNK, SGU_CHUNK), 0)
    col = lax.broadcasted_iota(jnp.int32, (SGU_CHUNK, SGU_CHUNK), 1)
    causal = row >= col
    sb = sb_ref[...]
    for hh in range(SGU_HEADS):
        w_h = jnp.where(causal, sw_ref[hh], 0.0).astype(BF16)
        b_h = sb[:, hh:hh + 1]
        for cc in range(tm // SGU_CHUNK):
            rs = slice(cc * SGU_CHUNK, (cc + 1) * SGU_CHUNK)
            cs = slice(hh * hd, (hh + 1) * hd)
            mixed = jnp.dot(w_h, vln[rs, cs].astype(BF16), preferred_element_type=F32) + b_h
            ysgu_ref[0, rs, cs] = u[rs, cs] * mixed

    o0 = 2 * sgu_w
    qkv_ref[0] = p[:, o0:o0 + 2 * qk_w + v_w]
    o1 = o0 + 2 * qk_w + v_w
    z_ref[0] = p[:, o1:o1 + v_w]
    ba_ref[0] = p[:, o1 + v_w:o1 + v_w + LANES]


def _inproj_call(x, sh, sc, g, w_in_p, ln_g, ln_b, sgu_w_l, sgu_b_t, *, tm, sgu_w, qk_w, v_w):
    bsz, t, d = x.shape
    in_pad = w_in_p.shape[1]
    conv_c = 2 * qk_w + v_w
    kern = functools.partial(_inproj_kernel, sgu_w=sgu_w, qk_w=qk_w, v_w=v_w)
    row_spec = lambda width: pl.BlockSpec((1, tm, width), lambda b, i: (b, i, 0))
    mod_spec = pl.BlockSpec((1, 1, d), lambda b, i: (b, 0, 0))
    const2 = lambda shape: pl.BlockSpec(shape, lambda b, i: (0,) * len(shape))
    return pl.pallas_call(
        kern,
        grid=(bsz, t // tm),
        in_specs=[
            row_spec(d), mod_spec, mod_spec, const2((1, d)),
            const2((d, in_pad)), const2((1, sgu_w)), const2((1, sgu_w)),
            const2((SGU_HEADS, SGU_CHUNK, SGU_CHUNK)), const2((SGU_CHUNK, LANES)),
        ],
        out_specs=[row_spec(sgu_w), row_spec(conv_c), row_spec(v_w), row_spec(LANES)],
        out_shape=[
            jax.ShapeDtypeStruct((bsz, t, sgu_w), F32),
            jax.ShapeDtypeStruct((bsz, t, conv_c), F32),
            jax.ShapeDtypeStruct((bsz, t, v_w), F32),
            jax.ShapeDtypeStruct((bsz, t, LANES), F32),
        ],
        compiler_params=pltpu.CompilerParams(
            dimension_semantics=("arbitrary", "arbitrary"), vmem_limit_bytes=VMEM_LIMIT),
        name="inproj_sgu",
    )(x, sh, sc, g, w_in_p, ln_g, ln_b, sgu_w_l, sgu_b_t)


def _gdn_kernel(qkv_ref, z_ref, ba_ref, cw_ref, alog_ref, dtb_ref, ng_ref, o_ref,
                xe_ref, s_ref, *, tg):
    c = GDN_CHUNK
    nh = GDN_HEADS
    qk_w = nh * GDN_DK
    i = pl.program_id(1)

    @pl.when(i == 0)
    def _():
        xe_ref[0:SUBLANES, :] = jnp.zeros((SUBLANES, xe_ref.shape[1]), F32)
        s_ref[...] = jnp.zeros(s_ref.shape, F32)

    xe_ref[SUBLANES:SUBLANES + tg, :] = qkv_ref[0]
    cw = cw_ref[...]
    acc = None
    for j in range(CONV_WIDTH):
        off = SUBLANES - (CONV_WIDTH - 1) + j
        term = xe_ref[off:off + tg, :] * cw[j:j + 1, :]
        acc = term if acc is None else acc + term
    xe_ref[0:SUBLANES, :] = xe_ref[tg:tg + SUBLANES, :]
    qkv = acc * jax.nn.sigmoid(acc)

    ba = ba_ref[0]
    lane = lax.broadcasted_iota(jnp.int32, ba.shape, 1)
    beta_all = jax.nn.sigmoid(ba)
    sp_in = ba + dtb_ref[...]
    softplus = jnp.maximum(sp_in, 0.0) + jnp.log1p(jnp.exp(-jnp.abs(sp_in)))
    g_all = -jnp.exp(alog_ref[...]) * softplus
    gates = jnp.where(lane < nh, beta_all, jnp.where(lane < 2 * nh, g_all, 0.0))

    r = lax.broadcasted_iota(jnp.int32, (tg, tg), 0)
    s = lax.broadcasted_iota(jnp.int32, (tg, tg), 1)
    tri = jnp.where((r >= s) & ((r // c) == (s // c)), 1.0, 0.0).astype(BF16)
    g_hi, g_mid, g_lo = _split3(gates)
    d = functools.partial(jnp.dot, preferred_element_type=F32)
    gc = d(tri, g_hi) + (d(tri, g_mid) + d(tri, g_lo))
    gc_t = gc.T

    rc = lax.broadcasted_iota(jnp.int32, (c, c), 0)
    sc = lax.broadcasted_iota(jnp.int32, (c, c), 1)
    incl = rc >= sc
    eye = jnp.where(rc == sc, 1.0, 0.0)
    n_lvl = c.bit_length() - 1
    lvl_masks = []
    for j in range(n_lvl):
        m = ((rc >> (j + 1)) == (sc >> (j + 1))) & (((rc >> j) & 1) == 1) & (((sc >> j) & 1) == 0)
        lvl_masks.append(m)

    ng = ng_ref[...]
    for cc in range(tg // c):
        rs = slice(cc * c, (cc + 1) * c)
        for hh in range(nh):
            qh = qkv[rs, hh * GDN_DK:(hh + 1) * GDN_DK]
            kh = qkv[rs, qk_w + hh * GDN_DK:qk_w + (hh + 1) * GDN_DK]
            vh = qkv[rs, 2 * qk_w + hh * GDN_DV:2 * qk_w + (hh + 1) * GDN_DV]
            qh = qh * lax.rsqrt(jnp.sum(qh * qh, axis=-1, keepdims=True) + RMS_EPS) * (GDN_DK ** -0.5)
            kh = kh * lax.rsqrt(jnp.sum(kh * kh, axis=-1, keepdims=True) + RMS_EPS)

            beta = gates[rs, hh:hh + 1]
            gcol = gc[rs, nh + hh:nh + hh + 1]
            grow = gc_t[nh + hh:nh + hh + 1, rs]
            decay = jnp.where(incl, jnp.exp(jnp.where(incl, gcol - grow, 0.0)), 0.0)
            gamma = jnp.exp(gcol)
            g_last = gcol[c - 1:c, :]

            kq = _bdot_nt(jnp.concatenate([kh, qh], axis=0), kh)
            kk = kq[:c]
            qk = kq[c:] * decay
            m_mat = beta * kk * decay

            x_inv = eye - jnp.where(lvl_masks[0], m_mat, 0.0)
            for j in range(1, n_lvl):
                a_j = jnp.where(lvl_masks[j], m_mat, 0.0)
                x_inv = x_inv - _dot3(x_inv, _dot3(a_j, x_inv))

            rhs = jnp.concatenate([beta * vh, (beta * gamma) * kh], axis=-1)
            sol = _bdot(x_inv, rhs)
            u_new = sol[:, :GDN_DV]
            w_k = sol[:, GDN_DV:]

            st = s_ref[hh]
            q_dec = qh * gamma
            k_dec = kh * jnp.exp(g_last - gcol)
            ws = _bdot(jnp.concatenate([w_k, q_dec], axis=0), st)
            w = u_new - ws[:c]
            o = ws[c:] + _bdot(qk, w)
            s_ref[hh] = jnp.exp(g_last) * st + _bdot_tn(k_dec, w)

            zh = z_ref[0, rs, hh * GDN_DV:(hh + 1) * GDN_DV]
            o = o * lax.rsqrt(jnp.mean(o * o, axis=-1, keepdims=True) + RMS_EPS) * ng
            o_ref[0, rs, hh * GDN_DV:(hh + 1) * GDN_DV] = o * (zh * jax.nn.sigmoid(zh))


def _gdn_call(qkv, z, ba, conv_w, alog_p, dtb_p, norm_g, *, tg):
    bsz, t, conv_c = qkv.shape
    v_w = z.shape[-1]
    kern = functools.partial(_gdn_kernel, tg=tg)
    row_spec = lambda width: pl.BlockSpec((1, tg, width), lambda b, i: (b, i, 0))
    const2 = lambda shape: pl.BlockSpec(shape, lambda b, i: (0,) * len(shape))
    return pl.pallas_call(
        kern,
        grid=(bsz, t // tg),
        in_specs=[
            row_spec(conv_c), row_spec(v_w), row_spec(LANES),
            const2((CONV_WIDTH, conv_c)), const2((1, LANES)), const2((1, LANES)), const2((1, GDN_DV)),
        ],
        out_specs=row_spec(v_w),
        out_shape=jax.ShapeDtypeStruct((bsz, t, v_w), F32),
        scratch_shapes=[
            pltpu.VMEM((tg + SUBLANES, conv_c), F32),
            pltpu.VMEM((GDN_HEADS, GDN_DK, GDN_DV), F32),
        ],
        compiler_params=pltpu.CompilerParams(
            dimension_semantics=("arbitrary", "arbitrary"), vmem_limit_bytes=VMEM_LIMIT),
        name="gdn",
    )(qkv, z, ba, conv_w, alog_p, dtb_p, norm_g)


def _mlp_kernel(x_ref, ysgu_ref, ygdn_ref, g1_ref, sh_ref, sc_ref, g2_ref, n2_ref, wo_ref,
                w1_ref, w2_ref, fg_ref, o_ref, *, ff_tile, final):
    x = x_ref[0]
    sgu_w = ysgu_ref.shape[-1]
    y = _bdot(ysgu_ref[0], wo_ref[:sgu_w, :]) + _bdot(ygdn_ref[0], wo_ref[sgu_w:, :])
    x1 = x + g1_ref[0] * y
    ms = jnp.mean(x1 * x1, axis=-1, keepdims=True)
    h = x1 * lax.rsqrt(ms + RMS_EPS) * n2_ref[...]
    h = (h * (1.0 + sc_ref[0]) + sh_ref[0]).astype(BF16)
    d_ff = w1_ref.shape[1]
    acc = None
    for j in range(d_ff // ff_tile):
        cs = slice(j * ff_tile, (j + 1) * ff_tile)
        a = jnp.dot(h, w1_ref[:, cs], preferred_element_type=F32)
        a = jnp.square(jnp.maximum(a, 0.0))
        part = jnp.dot(a.astype(BF16), w2_ref[cs, :], preferred_element_type=F32)
        acc = part if acc is None else acc + part
    x2 = x1 + g2_ref[0] * acc
    if final:
        ms2 = jnp.mean(x2 * x2, axis=-1, keepdims=True)
        x2 = x2 * lax.rsqrt(ms2 + RMS_EPS) * fg_ref[...]
    o_ref[0] = x2


def _mlp_call(x, ysgu, ygdn, g1, sh2, sc2, g2, n2, w_out, w1, w2, final_g, *, tm, ff_tile, final):
    bsz, t, d = x.shape
    mix_w = w_out.shape[0]
    d_ff = w1.shape[1]
    kern = functools.partial(_mlp_kernel, ff_tile=ff_tile, final=final)
    row_spec = lambda width: pl.BlockSpec((1, tm, width), lambda b, i: (b, i, 0))
    mod_spec = pl.BlockSpec((1, 1, d), lambda b, i: (b, 0, 0))
    const2 = lambda shape: pl.BlockSpec(shape, lambda b, i: (0,) * len(shape),
                                        pipeline_mode=pl.Buffered(1))
    return pl.pallas_call(
        kern,
        grid=(bsz, t // tm),
        in_specs=[
            row_spec(d), row_spec(ysgu.shape[-1]), row_spec(ygdn.shape[-1]),
            mod_spec, mod_spec, mod_spec, mod_spec, const2((1, d)),
            const2((mix_w, d)), const2((d, d_ff)), const2((d_ff, d)), const2((1, d)),
        ],
        out_specs=row_spec(d),
        out_shape=jax.ShapeDtypeStruct((bsz, t, d), F32),
        compiler_params=pltpu.CompilerParams(
            dimension_semantics=("arbitrary", "arbitrary"), vmem_limit_bytes=VMEM_LIMIT),
        name="outproj_mlp",
    )(x, ysgu, ygdn, g1, sh2, sc2, g2, n2, w_out, w1, w2, final_g)


def _pick_tile(t, want):
    tile = min(t, want)
    assert t % tile == 0
    return tile


def kernel(x, c, w_ada, b_ada, norm1_g, w_in, sgu_ln_g, sgu_ln_b, sgu_w, sgu_b, conv_w, a_log, dt_bias,
           gdn_norm_g, w_out, norm2_g, w_ff1, w_ff2, final_g):
    depth = w_ada.shape[0]
    bsz, t, d = x.shape
    sgu_width = sgu_ln_g.shape[-1]
    v_w = GDN_HEADS * GDN_DV
    qk_w = GDN_HEADS * GDN_DK
    in_w = w_in.shape[-1]
    gate_off = 2 * sgu_width + 2 * qk_w + 2 * v_w
    assert in_w == gate_off + 2 * GDN_HEADS
    assert t % SGU_CHUNK == 0 and t % GDN_CHUNK == 0

    tm = _pick_tile(t, 512)
    tg = _pick_tile(t, 256)

    mod = _ada_call(c, w_ada, b_ada)
    mod = mod.reshape(depth, bsz, N_MOD, 1, d)

    w_in_p = jnp.pad(w_in, ((0, 0), (0, 0), (0, LANES - 2 * GDN_HEADS))).astype(BF16)
    w_out_b = w_out.astype(BF16)
    w1_b = w_ff1.astype(BF16)
    w2_b = w_ff2.astype(BF16)
    sgu_b_t = jnp.pad(jnp.swapaxes(sgu_b, 1, 2), ((0, 0), (0, 0), (0, LANES - SGU_HEADS)))
    lane_pad = (GDN_HEADS, LANES - 2 * GDN_HEADS)
    alog_p = jnp.pad(a_log, ((0, 0), lane_pad)).reshape(depth, 1, LANES)
    dtb_p = jnp.pad(dt_bias, ((0, 0), lane_pad)).reshape(depth, 1, LANES)

    for l in range(depth):
        sh1, sc1, g1, sh2, sc2, g2 = (mod[l, :, k] for k in range(N_MOD))
        ysgu, qkv, z, ba = _inproj_call(
            x, sh1, sc1, norm1_g[l].reshape(1, d), w_in_p[l],
            sgu_ln_g[l].reshape(1, -1), sgu_ln_b[l].reshape(1, -1), sgu_w[l], sgu_b_t[l],
            tm=tm, sgu_w=sgu_width, qk_w=qk_w, v_w=v_w)
        ygdn = _gdn_call(qkv, z, ba, conv_w[l], alog_p[l], dtb_p[l], gdn_norm_g[l].reshape(1, -1), tg=tg)
        x = _mlp_call(x, ysgu, ygdn, g1, sh2, sc2, g2, norm2_g[l].reshape(1, d),
                      w_out_b[l], w1_b[l], w2_b[l], final_g.reshape(1, d),
                      tm=tm, ff_tile=1024, final=(l == depth - 1))
    return x
```

```python
import functools

import jax
import jax.numpy as jnp
from jax import lax
from jax.experimental import pallas as pl
from jax.experimental.pallas import tpu as pltpu

F32 = jnp.float32
BF16 = jnp.bfloat16

RMS_EPS = 1e-6
LN_EPS = 1e-5
N_MOD = 6
SGU_HEADS = 4
SGU_CHUNK = 128
GDN_HEADS = 4
GDN_DK = 128
GDN_DV = 128
GDN_CHUNK = 64
CONV_WIDTH = 4
GDN_BLOCK = 128
GDN_INV_PASSES = 1

LANES = 128
SUBLANES = 8
VMEM_LIMIT = 56 * 1024 * 1024


def _bdot(a, b):
    return jnp.dot(a.astype(BF16), b.astype(BF16), preferred_element_type=F32)


def _bdot_nt(a, b):
    return lax.dot_general(a.astype(BF16), b.astype(BF16), (((1,), (1,)), ((), ())),
                           preferred_element_type=F32)


def _bdot_tn(a, b):
    return lax.dot_general(a.astype(BF16), b.astype(BF16), (((0,), (0,)), ((), ())),
                           preferred_element_type=F32)


def _split3(a):
    hi = a.astype(BF16)
    r1 = a - hi.astype(F32)
    mid = r1.astype(BF16)
    lo = (r1 - mid.astype(F32)).astype(BF16)
    return hi, mid, lo


def _dot3(a, b):
    a_hi = a.astype(BF16)
    a_lo = (a - a_hi.astype(F32)).astype(BF16)
    b_hi = b.astype(BF16)
    b_lo = (b - b_hi.astype(F32)).astype(BF16)
    d = functools.partial(jnp.dot, preferred_element_type=F32)
    return d(a_hi, b_hi) + (d(a_hi, b_lo) + d(a_lo, b_hi))


def _ada_kernel(c_ref, w_ref, b_ref, o_ref):
    c = c_ref[...]
    c_act = c * jax.nn.sigmoid(c)
    o_ref[0] = _bdot(c_act, w_ref[0]) + b_ref[0]


def _ada_call(c, w_ada, b_ada):
    depth, d, n = w_ada.shape
    bsz = c.shape[0]
    tn = 1536
    assert n % tn == 0
    return pl.pallas_call(
        _ada_kernel,
        grid=(depth, n // tn),
        in_specs=[
            pl.BlockSpec((bsz, d), lambda l, j: (0, 0)),
            pl.BlockSpec((1, d, tn), lambda l, j: (l, 0, j)),
            pl.BlockSpec((1, 1, tn), lambda l, j: (l, 0, j)),
        ],
        out_specs=pl.BlockSpec((1, bsz, tn), lambda l, j: (l, 0, j)),
        out_shape=jax.ShapeDtypeStruct((depth, bsz, n), F32),
        compiler_params=pltpu.CompilerParams(
            dimension_semantics=("arbitrary", "arbitrary"), vmem_limit_bytes=VMEM_LIMIT),
        name="ada_mod",
    )(c, w_ada, b_ada.reshape(depth, 1, n))


def _inproj_kernel(x_ref, sh_ref, sc_ref, g_ref, w_ref, lng_ref, lnb_ref, sw_ref, sb_ref,
                   ysgu_ref, qkv_ref, z_ref, ba_ref, *, sgu_w, qk_w, v_w):
    x = x_ref[0]
    tm = x.shape[0]
    ms = jnp.mean(x * x, axis=-1, keepdims=True)
    h = x * lax.rsqrt(ms + RMS_EPS) * g_ref[...]
    h = h * (1.0 + sc_ref[0]) + sh_ref[0]
    p = _bdot(h, w_ref[...])

    u = jax.nn.gelu(p[:, :sgu_w])
    vs = jax.nn.gelu(p[:, sgu_w:2 * sgu_w])
    mu = jnp.mean(vs, axis=-1, keepdims=True)
    vc = vs - mu
    var = jnp.mean(vc * vc, axis=-1, keepdims=True)
    vln = vc * lax.rsqrt(var + LN_EPS) * lng_ref[...] + lnb_ref[...]

    hd = sgu_w // SGU_HEADS
    row = lax.broadcasted_iota(jnp.int32, (SGU_CHUNK, SGU_CHUNK), 0)
    col = lax.broadcasted_iota(jnp.int32, (SGU_CHUNK, SGU_CHUNK), 1)
    causal = row >= col
    sb = sb_ref[...]
    for hh in range(SGU_HEADS):
        w_h = jnp.where(causal, sw_ref[hh], 0.0).astype(BF16)
        b_h = sb[:, hh:hh + 1]
        for cc in range(tm // SGU_CHUNK):
            rs = slice(cc * SGU_CHUNK, (cc + 1) * SGU_CHUNK)
            cs = slice(hh * hd, (hh + 1) * hd)
            mixed = jnp.dot(w_h, vln[rs, cs].astype(BF16), preferred_element_type=F32) + b_h
            ysgu_ref[0, rs, cs] = u[rs, cs] * mixed

    o0 = 2 * sgu_w
    qkv_ref[0] = p[:, o0:o0 + 2 * qk_w + v_w]
    o1 = o0 + 2 * qk_w + v_w
    z_ref[0] = p[:, o1:o1 + v_w]
    ba_ref[0] = p[:, o1 + v_w:o1 + v_w + LANES]


def _inproj_call(x, sh, sc, g, w_in_p, ln_g, ln_b, sgu_w_l, sgu_b_t, *, tm, sgu_w, qk_w, v_w):
    bsz, t, d = x.shape
    in_pad = w_in_p.shape[1]
    conv_c = 2 * qk_w + v_w
    kern = functools.partial(_inproj_kernel, sgu_w=sgu_w, qk_w=qk_w, v_w=v_w)
    row_spec = lambda width: pl.BlockSpec((1, tm, width), lambda b, i: (b, i, 0))
    mod_spec = pl.BlockSpec((1, 1, d), lambda b, i: (b, 0, 0))
    const2 = lambda shape: pl.BlockSpec(shape, lambda b, i: (0,) * len(shape))
    return pl.pallas_call(
        kern,
        grid=(bsz, t // tm),
        in_specs=[
            row_spec(d), mod_spec, mod_spec, const2((1, d)),
            const2((d, in_pad)), const2((1, sgu_w)), const2((1, sgu_w)),
            const2((SGU_HEADS, SGU_CHUNK, SGU_CHUNK)), const2((SGU_CHUNK, LANES)),
        ],
        out_specs=[row_spec(sgu_w), row_spec(conv_c), row_spec(v_w), row_spec(LANES)],
        out_shape=[
            jax.ShapeDtypeStruct((bsz, t, sgu_w), F32),
            jax.ShapeDtypeStruct((bsz, t, conv_c), F32),
            jax.ShapeDtypeStruct((bsz, t, v_w), F32),
            jax.ShapeDtypeStruct((bsz, t, LANES), F32),
        ],
        compiler_params=pltpu.CompilerParams(
            dimension_semantics=("arbitrary", "arbitrary"), vmem_limit_bytes=VMEM_LIMIT),
        name="inproj_sgu",
    )(x, sh, sc, g, w_in_p, ln_g, ln_b, sgu_w_l, sgu_b_t)


def _gdn_kernel(qkv_ref, z_ref, ba_ref, cw_ref, alog_ref, dtb_ref, ng_ref, o_ref,
                xe_ref, s_ref, *, tg, c, inv_passes):
    nh = GDN_HEADS
    qk_w = nh * GDN_DK
    i = pl.program_id(1)

    @pl.when(i == 0)
    def _():
        xe_ref[0:SUBLANES, :] = jnp.zeros((SUBLANES, xe_ref.shape[1]), F32)
        s_ref[...] = jnp.zeros(s_ref.shape, F32)

    xe_ref[SUBLANES:SUBLANES + tg, :] = qkv_ref[0]
    cw = cw_ref[...]
    acc = None
    for j in range(CONV_WIDTH):
        off = SUBLANES - (CONV_WIDTH - 1) + j
        term = xe_ref[off:off + tg, :] * cw[j:j + 1, :]
        acc = term if acc is None else acc + term
    xe_ref[0:SUBLANES, :] = xe_ref[tg:tg + SUBLANES, :]
    qkv = acc * jax.nn.sigmoid(acc)

    ba = ba_ref[0]
    lane = lax.broadcasted_iota(jnp.int32, ba.shape, 1)
    beta_all = jax.nn.sigmoid(ba)
    sp_in = ba + dtb_ref[...]
    softplus = jnp.maximum(sp_in, 0.0) + jnp.log1p(jnp.exp(-jnp.abs(sp_in)))
    g_all = -jnp.exp(alog_ref[...]) * softplus
    gates = jnp.where(lane < nh, beta_all, jnp.where(lane < 2 * nh, g_all, 0.0))

    r = lax.broadcasted_iota(jnp.int32, (tg, tg), 0)
    s = lax.broadcasted_iota(jnp.int32, (tg, tg), 1)
    tri = jnp.where((r >= s) & ((r // c) == (s // c)), 1.0, 0.0).astype(BF16)
    g_hi, g_mid, g_lo = _split3(gates)
    d = functools.partial(jnp.dot, preferred_element_type=F32)
    gc = d(tri, g_hi) + (d(tri, g_mid) + d(tri, g_lo))
    gc_t = gc.T

    rc = lax.broadcasted_iota(jnp.int32, (c, c), 0)
    sc = lax.broadcasted_iota(jnp.int32, (c, c), 1)
    incl = rc >= sc
    eye = jnp.where(rc == sc, 1.0, 0.0)
    n_lvl = c.bit_length() - 1
    lvl_masks = [((rc >> (j + 1)) == (sc >> (j + 1))) & (((rc >> j) & 1) == 1) & (((sc >> j) & 1) == 0)
                 for j in range(n_lvl)]
    mm_inv = _dot3 if inv_passes == 3 else _bdot

    items = [(b, hh) for b in range(tg // c) for hh in range(nh)]

    prep = []
    for b, hh in items:
        rs = slice(b * c, (b + 1) * c)
        qh = qkv[rs, hh * GDN_DK:(hh + 1) * GDN_DK]
        kh = qkv[rs, qk_w + hh * GDN_DK:qk_w + (hh + 1) * GDN_DK]
        vh = qkv[rs, 2 * qk_w + hh * GDN_DV:2 * qk_w + (hh + 1) * GDN_DV]
        qh = qh * lax.rsqrt(jnp.sum(qh * qh, axis=-1, keepdims=True) + RMS_EPS) * (GDN_DK ** -0.5)
        kh = kh * lax.rsqrt(jnp.sum(kh * kh, axis=-1, keepdims=True) + RMS_EPS)
        beta = gates[rs, hh:hh + 1]
        gcol = gc[rs, nh + hh:nh + hh + 1]
        grow = gc_t[nh + hh:nh + hh + 1, rs]
        g_last = gcol[c - 1:c, :]
        decay = jnp.where(incl, jnp.exp(jnp.where(incl, gcol - grow, 0.0)), 0.0)
        gamma = jnp.exp(gcol)
        prep.append(dict(q=qh, k=kh, beta=beta, decay=decay, g_last=g_last,
                         rhs=jnp.concatenate([beta * vh, (beta * gamma) * kh], axis=-1),
                         q_dec=qh * gamma, kd_t=(kh * jnp.exp(g_last - gcol)).T))

    kq = [_bdot_nt(jnp.concatenate([p["k"], p["q"]], axis=0), p["k"]) for p in prep]
    m_mats = [p["beta"] * a[:c] * p["decay"] for p, a in zip(prep, kq)]
    qks = [a[c:] * p["decay"] for p, a in zip(prep, kq)]

    x_inv = [eye - jnp.where(lvl_masks[0], m, 0.0) for m in m_mats]
    for j in range(1, n_lvl):
        a_j = [jnp.where(lvl_masks[j], m, 0.0) for m in m_mats]
        y_j = [mm_inv(a, x) for a, x in zip(a_j, x_inv)]
        z_j = [mm_inv(x, y) for x, y in zip(x_inv, y_j)]
        x_inv = [x - zz for x, zz in zip(x_inv, z_j)]
    sols = [_bdot(x, p["rhs"]) for x, p in zip(x_inv, prep)]

    ng = ng_ref[...]
    st = [s_ref[hh] for hh in range(nh)]
    for b in range(tg // c):
        rs = slice(b * c, (b + 1) * c)
        idx = [b * nh + hh for hh in range(nh)]
        ws = [_bdot(jnp.concatenate([sols[n][:, GDN_DV:], prep[n]["q_dec"]], axis=0), st[hh])
              for hh, n in enumerate(idx)]
        w = [sols[n][:, :GDN_DV] - ws[hh][:c] for hh, n in enumerate(idx)]
        o = [ws[hh][c:] + _bdot(qks[n], w[hh]) for hh, n in enumerate(idx)]
        st = [jnp.exp(prep[n]["g_last"]) * st[hh] + _bdot(prep[n]["kd_t"], w[hh]) for hh, n in enumerate(idx)]
        for hh in range(nh):
            cs = slice(hh * GDN_DV, (hh + 1) * GDN_DV)
            zh = z_ref[0, rs, cs]
            oh = o[hh]
            oh = oh * lax.rsqrt(jnp.mean(oh * oh, axis=-1, keepdims=True) + RMS_EPS) * ng
            o_ref[0, rs, cs] = oh * (zh * jax.nn.sigmoid(zh))
    for hh in range(nh):
        s_ref[hh] = st[hh]


def _gdn_call(qkv, z, ba, conv_w, alog_p, dtb_p, norm_g, *, tg, c, inv_passes):
    bsz, t, conv_c = qkv.shape
    v_w = z.shape[-1]
    kern = functools.partial(_gdn_kernel, tg=tg, c=c, inv_passes=inv_passes)
    row_spec = lambda width: pl.BlockSpec((1, tg, width), lambda b, i: (b, i, 0))
    const2 = lambda shape: pl.BlockSpec(shape, lambda b, i: (0,) * len(shape))
    return pl.pallas_call(
        kern,
        grid=(bsz, t // tg),
        in_specs=[
            row_spec(conv_c), row_spec(v_w), row_spec(LANES),
            const2((CONV_WIDTH, conv_c)), const2((1, LANES)), const2((1, LANES)), const2((1, GDN_DV)),
        ],
        out_specs=row_spec(v_w),
        out_shape=jax.ShapeDtypeStruct((bsz, t, v_w), F32),
        scratch_shapes=[
            pltpu.VMEM((tg + SUBLANES, conv_c), F32),
            pltpu.VMEM((GDN_HEADS, GDN_DK, GDN_DV), F32),
        ],
        compiler_params=pltpu.CompilerParams(
            dimension_semantics=("arbitrary", "arbitrary"), vmem_limit_bytes=VMEM_LIMIT),
        name="gdn",
    )(qkv, z, ba, conv_w, alog_p, dtb_p, norm_g)


def _mlp_kernel(x_ref, ysgu_ref, ygdn_ref, g1_ref, sh_ref, sc_ref, g2_ref, n2_ref, wo_ref,
                w1_ref, w2_ref, fg_ref, o_ref, *, ff_tile, final):
    x = x_ref[0]
    sgu_w = ysgu_ref.shape[-1]
    y = _bdot(ysgu_ref[0], wo_ref[:sgu_w, :]) + _bdot(ygdn_ref[0], wo_ref[sgu_w:, :])
    x1 = x + g1_ref[0] * y
    ms = jnp.mean(x1 * x1, axis=-1, keepdims=True)
    h = x1 * lax.rsqrt(ms + RMS_EPS) * n2_ref[...]
    h = (h * (1.0 + sc_ref[0]) + sh_ref[0]).astype(BF16)
    d_ff = w1_ref.shape[1]
    acc = None
    for j in range(d_ff // ff_tile):
        cs = slice(j * ff_tile, (j + 1) * ff_tile)
        a = jnp.dot(h, w1_ref[:, cs], preferred_element_type=F32)
        a = jnp.square(jnp.maximum(a, 0.0))
        part = jnp.dot(a.astype(BF16), w2_ref[cs, :], preferred_element_type=F32)
        acc = part if acc is None else acc + part
    x2 = x1 + g2_ref[0] * acc
    if final:
        ms2 = jnp.mean(x2 * x2, axis=-1, keepdims=True)
        x2 = x2 * lax.rsqrt(ms2 + RMS_EPS) * fg_ref[...]
    o_ref[0] = x2


def _mlp_call(x, ysgu, ygdn, g1, sh2, sc2, g2, n2, w_out, w1, w2, final_g, *, tm, ff_tile, final):
    bsz, t, d = x.shape
    mix_w = w_out.shape[0]
    d_ff = w1.shape[1]
    kern = functools.partial(_mlp_kernel, ff_tile=ff_tile, final=final)
    row_spec = lambda width: pl.BlockSpec((1, tm, width), lambda b, i: (b, i, 0))
    mod_spec = pl.BlockSpec((1, 1, d), lambda b, i: (b, 0, 0))
    const2 = lambda shape: pl.BlockSpec(shape, lambda b, i: (0,) * len(shape),
                                        pipeline_mode=pl.Buffered(1))
    return pl.pallas_call(
        kern,
        grid=(bsz, t // tm),
        in_specs=[
            row_spec(d), row_spec(ysgu.shape[-1]), row_spec(ygdn.shape[-1]),
            mod_spec, mod_spec, mod_spec, mod_spec, const2((1, d)),
            const2((mix_w, d)), const2((d, d_ff)), const2((d_ff, d)), const2((1, d)),
        ],
        out_specs=row_spec(d),
        out_shape=jax.ShapeDtypeStruct((bsz, t, d), F32),
        compiler_params=pltpu.CompilerParams(
            dimension_semantics=("arbitrary", "arbitrary"), vmem_limit_bytes=VMEM_LIMIT),
        name="outproj_mlp",
    )(x, ysgu, ygdn, g1, sh2, sc2, g2, n2, w_out, w1, w2, final_g)


def _pick_tile(t, want):
    tile = min(t, want)
    assert t % tile == 0
    return tile


def kernel(x, c, w_ada, b_ada, norm1_g, w_in, sgu_ln_g, sgu_ln_b, sgu_w, sgu_b, conv_w, a_log, dt_bias,
           gdn_norm_g, w_out, norm2_g, w_ff1, w_ff2, final_g):
    depth = w_ada.shape[0]
    bsz, t, d = x.shape
    sgu_width = sgu_ln_g.shape[-1]
    v_w = GDN_HEADS * GDN_DV
    qk_w = GDN_HEADS * GDN_DK
    in_w = w_in.shape[-1]
    gate_off = 2 * sgu_width + 2 * qk_w + 2 * v_w
    assert in_w == gate_off + 2 * GDN_HEADS
    assert t % SGU_CHUNK == 0 and t % GDN_CHUNK == 0

    tm = _pick_tile(t, 512)
    tg = _pick_tile(t, 256)

    mod = _ada_call(c, w_ada, b_ada)
    mod = mod.reshape(depth, bsz, N_MOD, 1, d)

    w_in_p = jnp.pad(w_in, ((0, 0), (0, 0), (0, LANES - 2 * GDN_HEADS))).astype(BF16)
    w_out_b = w_out.astype(BF16)
    w1_b = w_ff1.astype(BF16)
    w2_b = w_ff2.astype(BF16)
    sgu_b_t = jnp.pad(jnp.swapaxes(sgu_b, 1, 2), ((0, 0), (0, 0), (0, LANES - SGU_HEADS)))
    lane_pad = (GDN_HEADS, LANES - 2 * GDN_HEADS)
    alog_p = jnp.pad(a_log, ((0, 0), lane_pad)).reshape(depth, 1, LANES)
    dtb_p = jnp.pad(dt_bias, ((0, 0), lane_pad)).reshape(depth, 1, LANES)

    for l in range(depth):
        sh1, sc1, g1, sh2, sc2, g2 = (mod[l, :, k] for k in range(N_MOD))
        ysgu, qkv, z, ba = _inproj_call(
            x, sh1, sc1, norm1_g[l].reshape(1, d), w_in_p[l],
            sgu_ln_g[l].reshape(1, -1), sgu_ln_b[l].reshape(1, -1), sgu_w[l], sgu_b_t[l],
            tm=tm, sgu_w=sgu_width, qk_w=qk_w, v_w=v_w)
        ygdn = _gdn_call(qkv, z, ba, conv_w[l], alog_p[l], dtb_p[l], gdn_norm_g[l].reshape(1, -1),
                         tg=tg, c=GDN_BLOCK, inv_passes=GDN_INV_PASSES)
        x = _mlp_call(x, ysgu, ygdn, g1, sh2, sc2, g2, norm2_g[l].reshape(1, d),
                      w_out_b[l], w1_b[l], w2_b[l], final_g.reshape(1, d),
                      tm=tm, ff_tile=1024, final=(l == depth - 1))
    return x
```

```python
import functools

import jax
import jax.numpy as jnp
from jax import lax
from jax.experimental import pallas as pl
from jax.experimental.pallas import tpu as pltpu

F32 = jnp.float32
BF16 = jnp.bfloat16

RMS_EPS = 1e-6
LN_EPS = 1e-5
N_MOD = 6
SGU_HEADS = 4
SGU_CHUNK = 128
GDN_HEADS = 4
GDN_DK = 128
GDN_DV = 128
CONV_WIDTH = 4
GDN_BLOCK = 128

ROW_CHUNK = 128
COL_CHUNK = 256

LANES = 128
SUBLANES = 8
VMEM_LIMIT = 56 * 1024 * 1024


def _bdot(a, b):
    return jnp.dot(a.astype(BF16), b.astype(BF16), preferred_element_type=F32)


def _bdot_nt(a, b):
    return lax.dot_general(a.astype(BF16), b.astype(BF16), (((1,), (1,)), ((), ())),
                           preferred_element_type=F32)


def _split3(a):
    hi = a.astype(BF16)
    r1 = a - hi.astype(F32)
    mid = r1.astype(BF16)
    lo = (r1 - mid.astype(F32)).astype(BF16)
    return hi, mid, lo


def _silu(a):
    return a * jax.nn.sigmoid(a)


def _ada_kernel(c_ref, w_ref, b_ref, o_ref):
    o_ref[0] = _bdot(_silu(c_ref[...]), w_ref[0]) + b_ref[0]


def _ada_call(c, w_ada, b_ada):
    depth, d, n = w_ada.shape
    bsz = c.shape[0]
    tn = 1536
    assert n % tn == 0
    return pl.pallas_call(
        _ada_kernel,
        grid=(depth, n // tn),
        in_specs=[
            pl.BlockSpec((bsz, d), lambda l, j: (0, 0)),
            pl.BlockSpec((1, d, tn), lambda l, j: (l, 0, j)),
            pl.BlockSpec((1, 1, tn), lambda l, j: (l, 0, j)),
        ],
        out_specs=pl.BlockSpec((1, bsz, tn), lambda l, j: (l, 0, j)),
        out_shape=jax.ShapeDtypeStruct((depth, bsz, n), F32),
        compiler_params=pltpu.CompilerParams(
            dimension_semantics=("arbitrary", "arbitrary"), vmem_limit_bytes=VMEM_LIMIT),
        name="ada_mod",
    )(c, w_ada, b_ada.reshape(depth, 1, n))


def _inproj_phases(x_ref, sh_ref, sc_ref, g_ref, w_ref, lng_ref, lnb_ref, sw_ref, sb_ref, cw_ref, alog_ref, dtb_ref,
                   ysgu_ref, z_ref, xe_ref, qkv_s, gates_s, seq_start, *, sgu_w, qk_w, v_w):
    tm = x_ref.shape[1]
    rows = [slice(r0, r0 + ROW_CHUNK) for r0 in range(0, tm, ROW_CHUNK)]
    o0 = 2 * sgu_w
    conv_c = 2 * qk_w + v_w
    o1 = o0 + conv_c

    h_parts = []
    for rs in rows:
        yield 0, 250
        x = x_ref[0, rs, :]
        ms = jnp.mean(x * x, axis=-1, keepdims=True)
        h = x * lax.rsqrt(ms + RMS_EPS) * g_ref[...]
        h_parts.append((h * (1.0 + sc_ref[0]) + sh_ref[0]).astype(BF16))
    h = jnp.concatenate(h_parts, axis=0)

    def proj(c0, c1):
        parts = []
        for p0 in range(c0, c1, COL_CHUNK):
            yield 512, 0
            parts.append(jnp.dot(h, w_ref[:, p0:min(p0 + COL_CHUNK, c1)], preferred_element_type=F32))
        return parts[0] if len(parts) == 1 else jnp.concatenate(parts, axis=-1)

    p_u = yield from proj(0, sgu_w)
    p_v = yield from proj(sgu_w, o0)
    u_parts, vln_parts = [], []
    for rs in rows:
        yield 0, 200
        u_parts.append(jax.nn.gelu(p_u[rs]))
        yield 0, 320
        vs = jax.nn.gelu(p_v[rs])
        mu = jnp.mean(vs, axis=-1, keepdims=True)
        vc = vs - mu
        var = jnp.mean(vc * vc, axis=-1, keepdims=True)
        vln_parts.append((vc * lax.rsqrt(var + LN_EPS) * lng_ref[...] + lnb_ref[...]).astype(BF16))
    hd = sgu_w // SGU_HEADS
    row = lax.broadcasted_iota(jnp.int32, (SGU_CHUNK, SGU_CHUNK), 0)
    col = lax.broadcasted_iota(jnp.int32, (SGU_CHUNK, SGU_CHUNK), 1)
    causal = row >= col
    sb = sb_ref[...]
    assert ROW_CHUNK == SGU_CHUNK
    for hh in range(SGU_HEADS):
        yield 130, 70
        w_h = jnp.where(causal, sw_ref[hh], 0.0).astype(BF16)
        b_h = sb[:, hh:hh + 1]
        cs = slice(hh * hd, (hh + 1) * hd)
        for rs, u_c, vln_c in zip(rows, u_parts, vln_parts):
            mixed = jnp.dot(w_h, vln_c[:, cs], preferred_element_type=F32) + b_h
            ysgu_ref[0, rs, cs] = (u_c[:, cs] * mixed).astype(ysgu_ref.dtype)

    xe_ref[0:SUBLANES, :] = jnp.where(seq_start, 0.0, xe_ref[tm:tm + SUBLANES, :])
    cw = cw_ref[...]
    for part, width in enumerate((qk_w, qk_w, v_w)):
        c0 = part * qk_w
        cs_all = slice(c0, c0 + width)
        xe_ref[SUBLANES:SUBLANES + tm, cs_all] = yield from proj(o0 + c0, o0 + c0 + width)
        for rs in rows:
            base = SUBLANES + rs.start
            for hh in range(GDN_HEADS):
                yield 0, (100 if part == 2 else 125)
                cs = slice(c0 + hh * GDN_DK, c0 + (hh + 1) * GDN_DK)
                acc = xe_ref[base:base + ROW_CHUNK, cs] * cw[CONV_WIDTH - 1:CONV_WIDTH, cs]
                for j in range(CONV_WIDTH - 1):
                    off = base - (CONV_WIDTH - 1) + j
                    acc = acc + xe_ref[off:off + ROW_CHUNK, cs] * cw[j:j + 1, cs]
                act = _silu(acc)
                if part < 2:
                    inv = lax.rsqrt(jnp.sum(act * act, axis=-1, keepdims=True) + RMS_EPS)
                    act = act * (inv * (GDN_DK ** -0.5) if part == 0 else inv)
                qkv_s[rs, cs] = act.astype(qkv_s.dtype)

    p_zg = yield from proj(o1, w_ref.shape[1])
    z_ref[0] = p_zg[:, :v_w]

    yield 0, 250
    ba = p_zg[:, v_w:]
    lane = lax.broadcasted_iota(jnp.int32, ba.shape, 1)
    sp_in = ba + dtb_ref[...]
    softplus = jnp.maximum(sp_in, 0.0) + jnp.log1p(jnp.exp(-jnp.abs(sp_in)))
    g_all = -jnp.exp(alog_ref[...]) * softplus
    gates_s[...] = jnp.where(lane < GDN_HEADS, jax.nn.sigmoid(ba),
                             jnp.where(lane < 2 * GDN_HEADS, g_all, 0.0))


def _gdn_phases(qkv_s, gates_s, og_ref, s_ref, seq_start, *, tm, tg, c):
    nh = GDN_HEADS
    qk_w = nh * GDN_DK
    yield 0, 100
    gates_all = gates_s[...]
    staged = []
    for blk in range(tm // c):
        ra = slice(blk * c, (blk + 1) * c)
        staged.append([(qkv_s[ra, hh * GDN_DK:(hh + 1) * GDN_DK],
                        qkv_s[ra, qk_w + hh * GDN_DK:qk_w + (hh + 1) * GDN_DK],
                        qkv_s[ra, 2 * qk_w + hh * GDN_DV:2 * qk_w + (hh + 1) * GDN_DV]) for hh in range(nh)])
    st = [jnp.where(seq_start, 0.0, s_ref[hh]) for hh in range(nh)]

    r = lax.broadcasted_iota(jnp.int32, (tg, tg), 0)
    s = lax.broadcasted_iota(jnp.int32, (tg, tg), 1)
    tri = jnp.where((r >= s) & ((r // c) == (s // c)), 1.0, 0.0).astype(BF16)
    rc = lax.broadcasted_iota(jnp.int32, (c, c), 0)
    sc = lax.broadcasted_iota(jnp.int32, (c, c), 1)
    incl = rc >= sc
    eye = jnp.where(rc == sc, 1.0, 0.0)
    n_lvl = c.bit_length() - 1
    lvl_masks = [((rc >> (j + 1)) == (sc >> (j + 1))) & (((rc >> j) & 1) == 1) & (((sc >> j) & 1) == 0)
                 for j in range(n_lvl)]
    d = functools.partial(jnp.dot, preferred_element_type=F32)

    for grp in range(tm // tg):
        g0 = grp * tg
        yield 100, 200
        gates = gates_all[g0:g0 + tg, :]
        g_hi, g_mid, g_lo = _split3(gates)
        gc = d(tri, g_hi) + (d(tri, g_mid) + d(tri, g_lo))
        gc_t = gc.T

        items = [(b, hh) for b in range(tg // c) for hh in range(nh)]
        prep = []
        for b, hh in items:
            if hh == 0:
                yield 0, 600
            rs = slice(b * c, (b + 1) * c)
            qh, kh, vh = staged[g0 // c + b][hh]
            kf = kh.astype(F32)
            beta = gates[rs, hh:hh + 1]
            gcol = gc[rs, nh + hh:nh + hh + 1]
            grow = gc_t[nh + hh:nh + hh + 1, rs]
            g_last = gcol[c - 1:c, :]
            decay = jnp.where(incl, jnp.exp(jnp.where(incl, gcol - grow, 0.0)), 0.0)
            gamma = jnp.exp(gcol)
            prep.append(dict(
                kq_lhs=jnp.concatenate([kh, qh], axis=0), k=kh, beta=beta, decay=decay, g_last=g_last,
                rhs=jnp.concatenate([(beta * vh.astype(F32)).astype(BF16), ((beta * gamma) * kf).astype(BF16)],
                                    axis=-1),
                q_dec=(qh.astype(F32) * gamma).astype(BF16),
                kd_t=(kf * jnp.exp(g_last - gcol)).T.astype(BF16)))

        yield 400, 0
        kq = [_bdot_nt(p["kq_lhs"], p["k"]) for p in prep]
        yield 0, 200
        m_mats = [p["beta"] * a[:c] * p["decay"] for p, a in zip(prep, kq)]
        qks = [(a[c:] * p["decay"]).astype(BF16) for p, a in zip(prep, kq)]

        x_inv = [eye - jnp.where(lvl_masks[0], m, 0.0) for m in m_mats]
        half = len(items) // 2
        for j in range(1, n_lvl):
            x_b, y_j = [], []
            for lo in (0, half):
                yield 128, 50
                x_b += [x.astype(BF16) for x in x_inv[lo:lo + half]]
                y_j += [_bdot(jnp.where(lvl_masks[j], m, 0.0), x)
                        for m, x in zip(m_mats[lo:lo + half], x_b[lo:lo + half])]
            x_new = []
            for lo in (0, half):
                yield 128, 50
                x_new += [x - _bdot(xb, y) for x, xb, y in
                          zip(x_inv[lo:lo + half], x_b[lo:lo + half], y_j[lo:lo + half])]
            x_inv = x_new
        yield 400, 0
        sols = [_bdot(x, p["rhs"]) for x, p in zip(x_inv, prep)]

        for b in range(tg // c):
            yield 500, 200
            ra = slice(g0 + b * c, g0 + (b + 1) * c)
            idx = [b * nh + hh for hh in range(nh)]
            ws = [_bdot(jnp.concatenate([sols[n][:, GDN_DV:].astype(BF16), prep[n]["q_dec"]], axis=0), st[hh])
                  for hh, n in enumerate(idx)]
            w_b = [(sols[n][:, :GDN_DV] - ws[hh][:c]).astype(BF16) for hh, n in enumerate(idx)]
            for hh, n in enumerate(idx):
                og_ref[0, ra, hh * GDN_DV:(hh + 1) * GDN_DV] = ws[hh][c:] + _bdot(qks[n], w_b[hh])
            st = [jnp.exp(prep[n]["g_last"]) * st[hh] + _bdot(prep[n]["kd_t"], w_b[hh])
                  for hh, n in enumerate(idx)]
    for hh in range(nh):
        s_ref[hh] = st[hh]


def _merge_phases(*stages):
    pending = {}
    for stage in stages:
        cost = next(stage, None)
        if cost is not None:
            pending[stage] = cost
    given = {stage: 0 for stage in pending}
    while pending:
        stage = min(pending, key=lambda g: given[g])
        given[stage] += sum(pending.pop(stage))
        cost = next(stage, None)
        if cost is not None:
            pending[stage] = cost


def _mix_kernel(x_ref, sh_ref, sc_ref, g_ref, w_ref, lng_ref, lnb_ref, sw_ref, sb_ref, cw_ref, alog_ref, dtb_ref,
                ysgu_ref, z_ref, og_ref, xe_ref, qkv_s, gates_s, s_ref,
                *, tm, tg, c, tiles_per_seq, n_tiles, sgu_w, qk_w, v_w):
    n = pl.program_id(0)

    @pl.when(n == 0)
    def _():
        xe_ref[...] = jnp.zeros(xe_ref.shape, xe_ref.dtype)
        qkv_s[...] = jnp.zeros(qkv_s.shape, qkv_s.dtype)
        gates_s[...] = jnp.zeros(gates_s.shape, gates_s.dtype)
        s_ref[...] = jnp.zeros(s_ref.shape, s_ref.dtype)

    gdn_start = jnp.maximum(n - 1, 0) % tiles_per_seq == 0
    in_start = jnp.minimum(n, n_tiles - 1) % tiles_per_seq == 0
    gdn = _gdn_phases(qkv_s, gates_s, og_ref, s_ref, gdn_start, tm=tm, tg=tg, c=c)
    inproj = _inproj_phases(x_ref, sh_ref, sc_ref, g_ref, w_ref, lng_ref, lnb_ref, sw_ref, sb_ref, cw_ref,
                            alog_ref, dtb_ref, ysgu_ref, z_ref, xe_ref, qkv_s, gates_s, in_start,
                            sgu_w=sgu_w, qk_w=qk_w, v_w=v_w)
    _merge_phases(gdn, inproj)


def _mix_call(x, sh, sc, g, w_in_p, ln_g, ln_b, sgu_w_l, sgu_b_t, conv_w, alog_p, dtb_p,
              *, tm, tg, c, sgu_w, qk_w, v_w):
    bsz, t, d = x.shape
    in_pad = w_in_p.shape[1]
    conv_c = 2 * qk_w + v_w
    tps = t // tm
    n_tiles = bsz * tps
    kern = functools.partial(_mix_kernel, tm=tm, tg=tg, c=c, tiles_per_seq=tps, n_tiles=n_tiles,
                             sgu_w=sgu_w, qk_w=qk_w, v_w=v_w)

    def in_idx(n):
        k = jnp.minimum(n, n_tiles - 1)
        return k // tps, k % tps

    def gdn_idx(n):
        k = jnp.maximum(n - 1, 0)
        return k // tps, k % tps

    in_row = lambda width: pl.BlockSpec((1, tm, width), lambda n: (*in_idx(n), 0))
    mod_spec = pl.BlockSpec((1, 1, d), lambda n: (in_idx(n)[0], 0, 0))
    const2 = lambda shape: pl.BlockSpec(shape, lambda n: (0,) * len(shape))
    return pl.pallas_call(
        kern,
        grid=(n_tiles + 1,),
        in_specs=[
            in_row(d), mod_spec, mod_spec, const2((1, d)),
            pl.BlockSpec((d, in_pad), lambda n: (0, 0), pipeline_mode=pl.Buffered(1)),
            const2((1, sgu_w)), const2((1, sgu_w)),
            const2((SGU_HEADS, SGU_CHUNK, SGU_CHUNK)), const2((SGU_CHUNK, LANES)),
            const2((CONV_WIDTH, conv_c)), const2((1, LANES)), const2((1, LANES)),
        ],
        out_specs=[in_row(sgu_w), in_row(v_w),
                   pl.BlockSpec((1, tm, v_w), lambda n: (*gdn_idx(n), 0))],
        out_shape=[
            jax.ShapeDtypeStruct((bsz, t, sgu_w), BF16),
            jax.ShapeDtypeStruct((bsz, t, v_w), F32),
            jax.ShapeDtypeStruct((bsz, t, v_w), F32),
        ],
        scratch_shapes=[
            pltpu.VMEM((tm + SUBLANES, conv_c), F32),
            pltpu.VMEM((tm, conv_c), BF16),
            pltpu.VMEM((tm, LANES), F32),
            pltpu.VMEM((GDN_HEADS, GDN_DK, GDN_DV), F32),
        ],
        compiler_params=pltpu.CompilerParams(
            dimension_semantics=("arbitrary",), vmem_limit_bytes=VMEM_LIMIT),
        name="token_mix",
    )(x, sh, sc, g, w_in_p, ln_g, ln_b, sgu_w_l, sgu_b_t, conv_w, alog_p, dtb_p)


def _mlp_kernel(x_ref, ysgu_ref, og_ref, z_ref, ng_ref, g1_ref, sh_ref, sc_ref, g2_ref, n2_ref, wo_ref,
                w1_ref, w2_ref, fg_ref, o_ref, *, ff_tile, final):
    x = x_ref[0]
    sgu_w = ysgu_ref.shape[-1]
    ng = ng_ref[...]
    ygdn = []
    for hh in range(GDN_HEADS):
        cs = slice(hh * GDN_DV, (hh + 1) * GDN_DV)
        oh = og_ref[0, :, cs]
        oh = oh * lax.rsqrt(jnp.mean(oh * oh, axis=-1, keepdims=True) + RMS_EPS) * ng
        ygdn.append((oh * _silu(z_ref[0, :, cs])).astype(BF16))
    y = (jnp.dot(ysgu_ref[0], wo_ref[:sgu_w, :], preferred_element_type=F32)
         + jnp.dot(jnp.concatenate(ygdn, axis=-1), wo_ref[sgu_w:, :], preferred_element_type=F32))
    x1 = x + g1_ref[0] * y
    ms = jnp.mean(x1 * x1, axis=-1, keepdims=True)
    h = x1 * lax.rsqrt(ms + RMS_EPS) * n2_ref[...]
    h = (h * (1.0 + sc_ref[0]) + sh_ref[0]).astype(BF16)
    d_ff = w1_ref.shape[1]
    acc = None
    for j in range(d_ff // ff_tile):
        cs = slice(j * ff_tile, (j + 1) * ff_tile)
        a = jnp.dot(h, w1_ref[:, cs], preferred_element_type=F32)
        a = jnp.square(jnp.maximum(a, 0.0))
        part = jnp.dot(a.astype(BF16), w2_ref[cs, :], preferred_element_type=F32)
        acc = part if acc is None else acc + part
    x2 = x1 + g2_ref[0] * acc
    if final:
        ms2 = jnp.mean(x2 * x2, axis=-1, keepdims=True)
        x2 = x2 * lax.rsqrt(ms2 + RMS_EPS) * fg_ref[...]
    o_ref[0] = x2


def _mlp_call(x, ysgu, og, z, ng, g1, sh2, sc2, g2, n2, w_out, w1, w2, final_g, *, tm, ff_tile, final):
    bsz, t, d = x.shape
    mix_w = w_out.shape[0]
    d_ff = w1.shape[1]
    kern = functools.partial(_mlp_kernel, ff_tile=ff_tile, final=final)
    row_spec = lambda width: pl.BlockSpec((1, tm, width), lambda b, i: (b, i, 0))
    mod_spec = pl.BlockSpec((1, 1, d), lambda b, i: (b, 0, 0))
    const2 = lambda shape: pl.BlockSpec(shape, lambda b, i: (0,) * len(shape),
                                        pipeline_mode=pl.Buffered(1))
    return pl.pallas_call(
        kern,
        grid=(bsz, t // tm),
        in_specs=[
            row_spec(d), row_spec(ysgu.shape[-1]), row_spec(og.shape[-1]), row_spec(z.shape[-1]),
            const2((1, GDN_DV)),
            mod_spec, mod_spec, mod_spec, mod_spec, const2((1, d)),
            const2((mix_w, d)), const2((d, d_ff)), const2((d_ff, d)), const2((1, d)),
        ],
        out_specs=row_spec(d),
        out_shape=jax.ShapeDtypeStruct((bsz, t, d), F32),
        compiler_params=pltpu.CompilerParams(
            dimension_semantics=("arbitrary", "arbitrary"), vmem_limit_bytes=VMEM_LIMIT),
        name="outproj_mlp",
    )(x, ysgu, og, z, ng, g1, sh2, sc2, g2, n2, w_out, w1, w2, final_g)


def _pick_tile(t, want):
    tile = min(t, want)
    assert t % tile == 0
    return tile


def kernel(x, c, w_ada, b_ada, norm1_g, w_in, sgu_ln_g, sgu_ln_b, sgu_w, sgu_b, conv_w, a_log, dt_bias,
           gdn_norm_g, w_out, norm2_g, w_ff1, w_ff2, final_g):
    depth = w_ada.shape[0]
    bsz, t, d = x.shape
    sgu_width = sgu_ln_g.shape[-1]
    v_w = GDN_HEADS * GDN_DV
    qk_w = GDN_HEADS * GDN_DK
    in_w = w_in.shape[-1]
    gate_off = 2 * sgu_width + 2 * qk_w + 2 * v_w
    assert in_w == gate_off + 2 * GDN_HEADS
    assert t % SGU_CHUNK == 0 and t % GDN_BLOCK == 0

    tm = _pick_tile(t, 512)
    tg = _pick_tile(t, 256)

    mod = _ada_call(c, w_ada, b_ada)
    mod = mod.reshape(depth, bsz, N_MOD, 1, d)

    w_in_p = jnp.pad(w_in, ((0, 0), (0, 0), (0, LANES - 2 * GDN_HEADS))).astype(BF16)
    w_out_b = w_out.astype(BF16)
    w1_b = w_ff1.astype(BF16)
    w2_b = w_ff2.astype(BF16)
    sgu_b_t = jnp.pad(jnp.swapaxes(sgu_b, 1, 2), ((0, 0), (0, 0), (0, LANES - SGU_HEADS)))
    lane_pad = (GDN_HEADS, LANES - 2 * GDN_HEADS)
    alog_p = jnp.pad(a_log, ((0, 0), lane_pad)).reshape(depth, 1, LANES)
    dtb_p = jnp.pad(dt_bias, ((0, 0), lane_pad)).reshape(depth, 1, LANES)

    for l in range(depth):
        sh1, sc1, g1, sh2, sc2, g2 = (mod[l, :, k] for k in range(N_MOD))
        ysgu, z, og = _mix_call(
            x, sh1, sc1, norm1_g[l].reshape(1, d), w_in_p[l],
            sgu_ln_g[l].reshape(1, -1), sgu_ln_b[l].reshape(1, -1), sgu_w[l], sgu_b_t[l],
            conv_w[l], alog_p[l], dtb_p[l],
            tm=tm, tg=tg, c=GDN_BLOCK, sgu_w=sgu_width, qk_w=qk_w, v_w=v_w)
        x = _mlp_call(x, ysgu, og, z, gdn_norm_g[l].reshape(1, -1), g1, sh2, sc2, g2,
                      norm2_g[l].reshape(1, d), w_out_b[l], w1_b[l], w2_b[l], final_g.reshape(1, d),
                      tm=tm, ff_tile=1024, final=(l == depth - 1))
    return x
```

```python
import functools

import jax
import jax.numpy as jnp
from jax import lax
from jax.experimental import pallas as pl
from jax.experimental.pallas import tpu as pltpu

F32 = jnp.float32
BF16 = jnp.bfloat16

RMS_EPS = 1e-6
LN_EPS = 1e-5
N_MOD = 6
MOD_SHIFT1, MOD_SCALE1, MOD_GATE1, MOD_SHIFT2, MOD_SCALE2, MOD_GATE2 = range(N_MOD)
SGU_HEADS = 4
SGU_CHUNK = 128
GDN_HEADS = 4
GDN_DK = 128
GDN_DV = 128
CONV_WIDTH = 4
GDN_BLOCK = 128

ROW_CHUNK = 128
COL_CHUNK = 256

LANES = 128
SUBLANES = 8
BF16_ROWS = 16
VMEM_LIMIT = 56 * 1024 * 1024


def _bdot(a, b):
    return jnp.dot(a.astype(BF16), b.astype(BF16), preferred_element_type=F32)


def _bdot_nt(a, b):
    return lax.dot_general(a.astype(BF16), b.astype(BF16), (((1,), (1,)), ((), ())),
                           preferred_element_type=F32)


def _split3(a):
    hi = a.astype(BF16)
    r1 = a - hi.astype(F32)
    mid = r1.astype(BF16)
    lo = (r1 - mid.astype(F32)).astype(BF16)
    return hi, mid, lo


_LOG2E = 1.4426950408889634
_GELU_C0 = 0.7978845608028654
_GELU_C1 = 0.044715 * _GELU_C0


def _silu(a):
    return a / (1.0 + jnp.exp2(a * (-_LOG2E)))


def _gelu_tanh(a):
    t = a * a
    e = jnp.exp2(a * ((-2.0 * _LOG2E * _GELU_C0) + (-2.0 * _LOG2E * _GELU_C1) * t))
    return a / (1.0 + e)


def _ada_kernel(c_ref, w_ref, b_ref, o_ref):
    o_ref[0] = _bdot(_silu(c_ref[...]), w_ref[0]) + b_ref[0]


def _ada_call(c, w_ada, b_ada):
    depth, d, n = w_ada.shape
    bsz = c.shape[0]
    tn = 1536
    assert n % tn == 0
    return pl.pallas_call(
        _ada_kernel,
        grid=(depth, n // tn),
        in_specs=[
            pl.BlockSpec((bsz, d), lambda l, j: (0, 0)),
            pl.BlockSpec((1, d, tn), lambda l, j: (l, 0, j)),
            pl.BlockSpec((1, 1, tn), lambda l, j: (l, 0, j)),
        ],
        out_specs=pl.BlockSpec((1, bsz, tn), lambda l, j: (l, 0, j)),
        out_shape=jax.ShapeDtypeStruct((depth, bsz, n), F32),
        compiler_params=pltpu.CompilerParams(
            dimension_semantics=("arbitrary", "arbitrary"), vmem_limit_bytes=VMEM_LIMIT),
        name="ada_mod",
    )(c, w_ada, b_ada.reshape(depth, 1, n))


def _inproj_phases(x_ref, sh_ref, sc_ref, g_ref, w_ref, lng_ref, lnb_ref, sw_ref, sb_ref, cw_ref, alog_ref, dtb_ref,
                   ysgu_ref, z_ref, xe_ref, qkv_s, gates_s, seq_start, *, sgu_w, qk_w, v_w):
    tm = x_ref.shape[1]
    rows = [slice(r0, r0 + ROW_CHUNK) for r0 in range(0, tm, ROW_CHUNK)]
    o0 = 2 * sgu_w
    conv_c = 2 * qk_w + v_w
    o1 = o0 + conv_c

    h_parts = []
    gain = g_ref[...] * (1.0 + sc_ref[0])
    for rs in rows:
        yield 0, 250
        x = x_ref[0, rs, :]
        ms = jnp.mean(x * x, axis=-1, keepdims=True)
        h_parts.append(((x * lax.rsqrt(ms + RMS_EPS)) * gain + sh_ref[0]).astype(BF16))
    h = jnp.concatenate(h_parts, axis=0)

    def proj(c0, c1):
        parts = []
        for p0 in range(c0, c1, COL_CHUNK):
            yield 512, 0
            parts.append(jnp.dot(h, w_ref[:, p0:min(p0 + COL_CHUNK, c1)], preferred_element_type=F32))
        return parts[0] if len(parts) == 1 else jnp.concatenate(parts, axis=-1)

    p_u = yield from proj(0, sgu_w)
    p_v = yield from proj(sgu_w, o0)
    u_parts, vln_parts = [], []
    for rs in rows:
        yield 0, 200
        u_parts.append(_gelu_tanh(p_u[rs]))
        yield 0, 320
        vs = _gelu_tanh(p_v[rs])
        mu = jnp.mean(vs, axis=-1, keepdims=True)
        vc = vs - mu
        var = jnp.mean(vc * vc, axis=-1, keepdims=True)
        vln_parts.append((vc * lax.rsqrt(var + LN_EPS) * lng_ref[...] + lnb_ref[...]).astype(BF16))
    hd = sgu_w // SGU_HEADS
    row = lax.broadcasted_iota(jnp.int32, (SGU_CHUNK, SGU_CHUNK), 0)
    col = lax.broadcasted_iota(jnp.int32, (SGU_CHUNK, SGU_CHUNK), 1)
    causal = row >= col
    sb = sb_ref[...]
    assert ROW_CHUNK == SGU_CHUNK
    for hh in range(SGU_HEADS):
        yield 130, 70
        w_h = jnp.where(causal, sw_ref[hh], 0.0).astype(BF16)
        b_h = sb[:, hh:hh + 1]
        cs = slice(hh * hd, (hh + 1) * hd)
        for rs, u_c, vln_c in zip(rows, u_parts, vln_parts):
            mixed = jnp.dot(w_h, vln_c[:, cs], preferred_element_type=F32) + b_h
            ysgu_ref[0, rs, cs] = (u_c[:, cs] * mixed).astype(ysgu_ref.dtype)

    xe_ref[0:SUBLANES, :] = jnp.where(seq_start, 0.0, xe_ref[tm:tm + SUBLANES, :])
    cw = cw_ref[...]
    for part, width in enumerate((qk_w, qk_w, v_w)):
        c0 = part * qk_w
        cs_all = slice(c0, c0 + width)
        xe_ref[SUBLANES:SUBLANES + tm, cs_all] = yield from proj(o0 + c0, o0 + c0 + width)
        for rs in rows:
            base = SUBLANES + rs.start
            for hh in range(GDN_HEADS):
                yield 0, (100 if part == 2 else 125)
                cs = slice(c0 + hh * GDN_DK, c0 + (hh + 1) * GDN_DK)
                acc = xe_ref[base:base + ROW_CHUNK, cs] * cw[CONV_WIDTH - 1:CONV_WIDTH, cs]
                for j in range(CONV_WIDTH - 1):
                    off = base - (CONV_WIDTH - 1) + j
                    acc = acc + xe_ref[off:off + ROW_CHUNK, cs] * cw[j:j + 1, cs]
                act = _silu(acc)
                if part < 2:
                    inv = lax.rsqrt(jnp.sum(act * act, axis=-1, keepdims=True) + RMS_EPS)
                    act = act * (inv * (GDN_DK ** -0.5) if part == 0 else inv)
                qkv_s[rs, cs] = act.astype(qkv_s.dtype)

    p_zg = yield from proj(o1, w_ref.shape[1])
    z_ref[0] = p_zg[:, :v_w]

    yield 0, 250
    ba = p_zg[:, v_w:]
    lane = lax.broadcasted_iota(jnp.int32, ba.shape, 1)
    sp_in = ba + dtb_ref[...]
    softplus = jnp.maximum(sp_in, 0.0) + jnp.log1p(jnp.exp(-jnp.abs(sp_in)))
    g_all = -jnp.exp(alog_ref[...]) * softplus
    gates_s[...] = jnp.where(lane < GDN_HEADS, jax.nn.sigmoid(ba),
                             jnp.where(lane < 2 * GDN_HEADS, g_all, 0.0))


def _gdn_phases(qkv_s, gates_s, og_ref, s_ref, seq_start, *, tm, tg, c):
    nh = GDN_HEADS
    qk_w = nh * GDN_DK
    yield 0, 100
    gates_all = gates_s[...]
    staged = []
    for blk in range(tm // c):
        ra = slice(blk * c, (blk + 1) * c)
        staged.append([(qkv_s[ra, hh * GDN_DK:(hh + 1) * GDN_DK],
                        qkv_s[ra, qk_w + hh * GDN_DK:qk_w + (hh + 1) * GDN_DK],
                        qkv_s[ra, 2 * qk_w + hh * GDN_DV:2 * qk_w + (hh + 1) * GDN_DV]) for hh in range(nh)])
    st = [jnp.where(seq_start, 0.0, s_ref[hh]) for hh in range(nh)]

    r = lax.broadcasted_iota(jnp.int32, (tg, tg), 0)
    s = lax.broadcasted_iota(jnp.int32, (tg, tg), 1)
    tri = jnp.where((r >= s) & ((r // c) == (s // c)), 1.0, 0.0).astype(BF16)
    rc = lax.broadcasted_iota(jnp.int32, (c, c), 0)
    sc = lax.broadcasted_iota(jnp.int32, (c, c), 1)
    incl = rc >= sc
    eye_b = jnp.where(rc == sc, 1.0, 0.0).astype(BF16)
    n_lvl = c.bit_length() - 1
    lvl_sel = [jnp.where(((rc >> (j + 1)) == (sc >> (j + 1))) & (((rc >> j) & 1) == 1) & (((sc >> j) & 1) == 0),
                         1.0, 0.0).astype(BF16) for j in range(n_lvl)]
    d = functools.partial(jnp.dot, preferred_element_type=F32)

    for grp in range(tm // tg):
        g0 = grp * tg
        yield 100, 200
        gates = gates_all[g0:g0 + tg, :]
        g_hi, g_mid, g_lo = _split3(gates)
        gc = d(tri, g_hi) + (d(tri, g_mid) + d(tri, g_lo))
        gc_t = gc.T

        items = [(b, hh) for b in range(tg // c) for hh in range(nh)]
        prep = []
        for b, hh in items:
            if hh == 0:
                yield 0, 600
            rs = slice(b * c, (b + 1) * c)
            qh, kh, vh = staged[g0 // c + b][hh]
            kf = kh.astype(F32)
            beta = gates[rs, hh:hh + 1]
            gcol = gc[rs, nh + hh:nh + hh + 1]
            grow = gc_t[nh + hh:nh + hh + 1, rs]
            g_last = gcol[c - 1:c, :]
            decay = jnp.where(incl, jnp.exp(jnp.where(incl, gcol - grow, 0.0)), 0.0)
            gamma = jnp.exp(gcol)
            prep.append(dict(
                kq_lhs=jnp.concatenate([kh, qh], axis=0), k=kh, beta=beta, decay=decay, g_last=g_last,
                rhs=jnp.concatenate([(beta * vh.astype(F32)).astype(BF16), ((beta * gamma) * kf).astype(BF16)],
                                    axis=-1),
                q_dec=(qh.astype(F32) * gamma).astype(BF16),
                kd_t=(kf * jnp.exp(g_last - gcol)).T.astype(BF16)))

        yield 400, 0
        kq = [_bdot_nt(p["kq_lhs"], p["k"]) for p in prep]
        yield 0, 200
        m_mats = [p["beta"] * a[:c] * p["decay"] for p, a in zip(prep, kq)]
        qks = [(a[c:] * p["decay"]).astype(BF16) for p, a in zip(prep, kq)]

        m_b = [m.astype(BF16) for m in m_mats]
        x_inv = [eye_b - m * lvl_sel[0] for m in m_b]
        for j in range(1, n_lvl):
            blk = 1 << j
            if blk >= BF16_ROWS:
                sel = [slice(r0, r0 + blk) for r0 in range(blk, c, 2 * blk)]
                rows_of = lambda a: jnp.concatenate([a[rs] for rs in sel], axis=0) if len(sel) > 1 else a[sel[0]]
                sel_mask = rows_of(lvl_sel[j])
                yield 128, 50
                y_j = [d(rows_of(m) * sel_mask, x).astype(BF16) for m, x in zip(m_b, x_inv)]
                zero = jnp.zeros((blk, c), BF16)
                y_full = [jnp.concatenate([piece for k in range(len(sel)) for piece in (zero, y[k * blk:(k + 1) * blk])],
                                          axis=0) for y in y_j]
                yield 128, 50
                x_new = []
                for x, y in zip(x_inv, y_full):
                    upd = rows_of(x) - d(rows_of(x), y).astype(BF16)
                    x_new.append(jnp.concatenate(
                        [piece for k, rs in enumerate(sel)
                         for piece in (x[rs.start - blk:rs.start], upd[k * blk:(k + 1) * blk])], axis=0))
                x_inv = x_new
            else:
                yield 256, 80
                y_j = [d(m * lvl_sel[j], x).astype(BF16) for m, x in zip(m_b, x_inv)]
                yield 256, 80
                x_inv = [x - d(x, y).astype(BF16) for x, y in zip(x_inv, y_j)]
        yield 400, 0
        sols = [d(x, p["rhs"]) for x, p in zip(x_inv, prep)]

        for b in range(tg // c):
            yield 500, 200
            ra = slice(g0 + b * c, g0 + (b + 1) * c)
            idx = [b * nh + hh for hh in range(nh)]
            ws = [_bdot(jnp.concatenate([sols[n][:, GDN_DV:].astype(BF16), prep[n]["q_dec"]], axis=0), st[hh])
                  for hh, n in enumerate(idx)]
            w_b = [(sols[n][:, :GDN_DV] - ws[hh][:c]).astype(BF16) for hh, n in enumerate(idx)]
            for hh, n in enumerate(idx):
                og_ref[0, ra, hh * GDN_DV:(hh + 1) * GDN_DV] = ws[hh][c:] + _bdot(qks[n], w_b[hh])
            st = [jnp.exp(prep[n]["g_last"]) * st[hh] + _bdot(prep[n]["kd_t"], w_b[hh])
                  for hh, n in enumerate(idx)]
    for hh in range(nh):
        s_ref[hh] = st[hh]


def _merge_phases(*stages):
    pending = {}
    for stage in stages:
        cost = next(stage, None)
        if cost is not None:
            pending[stage] = cost
    given = {stage: 0 for stage in pending}
    while pending:
        stage = min(pending, key=lambda g: given[g])
        given[stage] += sum(pending.pop(stage))
        cost = next(stage, None)
        if cost is not None:
            pending[stage] = cost


def _mix_kernel(x_ref, sh_ref, sc_ref, g_ref, w_ref, lng_ref, lnb_ref, sw_ref, sb_ref, cw_ref, alog_ref, dtb_ref,
                ysgu_ref, z_ref, og_ref, xe_ref, qkv_s, gates_s, s_ref,
                *, tm, tg, c, tiles_per_seq, n_tiles, sgu_w, qk_w, v_w):
    n = pl.program_id(0)

    @pl.when(n == 0)
    def _():
        xe_ref[...] = jnp.zeros(xe_ref.shape, xe_ref.dtype)
        qkv_s[...] = jnp.zeros(qkv_s.shape, qkv_s.dtype)
        gates_s[...] = jnp.zeros(gates_s.shape, gates_s.dtype)
        s_ref[...] = jnp.zeros(s_ref.shape, s_ref.dtype)

    gdn_start = jnp.maximum(n - 1, 0) % tiles_per_seq == 0
    in_start = jnp.minimum(n, n_tiles - 1) % tiles_per_seq == 0
    gdn = _gdn_phases(qkv_s, gates_s, og_ref, s_ref, gdn_start, tm=tm, tg=tg, c=c)
    inproj = _inproj_phases(x_ref, sh_ref, sc_ref, g_ref, w_ref, lng_ref, lnb_ref, sw_ref, sb_ref, cw_ref,
                            alog_ref, dtb_ref, ysgu_ref, z_ref, xe_ref, qkv_s, gates_s, in_start,
                            sgu_w=sgu_w, qk_w=qk_w, v_w=v_w)
    _merge_phases(gdn, inproj)


def _mix_call(x, mod, layer, g, w_in_p, ln_g, ln_b, sgu_w_l, sgu_b_t, conv_w, alog_p, dtb_p,
              *, tm, tg, c, sgu_w, qk_w, v_w):
    bsz, t, d = x.shape
    in_pad = w_in_p.shape[1]
    conv_c = 2 * qk_w + v_w
    tps = t // tm
    n_tiles = bsz * tps
    kern = functools.partial(_mix_kernel, tm=tm, tg=tg, c=c, tiles_per_seq=tps, n_tiles=n_tiles,
                             sgu_w=sgu_w, qk_w=qk_w, v_w=v_w)

    def in_idx(n):
        k = jnp.minimum(n, n_tiles - 1)
        return k // tps, k % tps

    def gdn_idx(n):
        k = jnp.maximum(n - 1, 0)
        return k // tps, k % tps

    in_row = lambda width: pl.BlockSpec((1, tm, width), lambda n: (*in_idx(n), 0))
    mod_spec = lambda k: pl.BlockSpec((None, 1, None, 1, d), lambda n: (layer, in_idx(n)[0], k, 0, 0))
    const2 = lambda shape: pl.BlockSpec(shape, lambda n: (0,) * len(shape))
    return pl.pallas_call(
        kern,
        grid=(n_tiles + 1,),
        in_specs=[
            in_row(d), mod_spec(MOD_SHIFT1), mod_spec(MOD_SCALE1), const2((1, d)),
            pl.BlockSpec((d, in_pad), lambda n: (0, 0), pipeline_mode=pl.Buffered(1)),
            const2((1, sgu_w)), const2((1, sgu_w)),
            const2((SGU_HEADS, SGU_CHUNK, SGU_CHUNK)), const2((SGU_CHUNK, LANES)),
            const2((CONV_WIDTH, conv_c)), const2((1, LANES)), const2((1, LANES)),
        ],
        out_specs=[in_row(sgu_w), in_row(v_w),
                   pl.BlockSpec((1, tm, v_w), lambda n: (*gdn_idx(n), 0))],
        out_shape=[
            jax.ShapeDtypeStruct((bsz, t, sgu_w), BF16),
            jax.ShapeDtypeStruct((bsz, t, v_w), F32),
            jax.ShapeDtypeStruct((bsz, t, v_w), F32),
        ],
        scratch_shapes=[
            pltpu.VMEM((tm + SUBLANES, conv_c), F32),
            pltpu.VMEM((tm, conv_c), BF16),
            pltpu.VMEM((tm, LANES), F32),
            pltpu.VMEM((GDN_HEADS, GDN_DK, GDN_DV), F32),
        ],
        compiler_params=pltpu.CompilerParams(
            dimension_semantics=("arbitrary",), vmem_limit_bytes=VMEM_LIMIT),
        name="token_mix",
    )(x, mod, mod, g, w_in_p, ln_g, ln_b, sgu_w_l, sgu_b_t, conv_w, alog_p, dtb_p)


def _mlp_kernel(x_ref, ysgu_ref, og_ref, z_ref, ng_ref, g1_ref, sh_ref, sc_ref, g2_ref, n2_ref, wo_ref,
                w1_ref, w2_ref, fg_ref, o_ref, *, ff_tile, final):
    x = x_ref[0]
    sgu_w = ysgu_ref.shape[-1]
    ng = ng_ref[...]
    ygdn = []
    for hh in range(GDN_HEADS):
        cs = slice(hh * GDN_DV, (hh + 1) * GDN_DV)
        oh = og_ref[0, :, cs]
        oh = oh * lax.rsqrt(jnp.mean(oh * oh, axis=-1, keepdims=True) + RMS_EPS) * ng
        ygdn.append((oh * _silu(z_ref[0, :, cs])).astype(BF16))
    y = (jnp.dot(ysgu_ref[0], wo_ref[:sgu_w, :], preferred_element_type=F32)
         + jnp.dot(jnp.concatenate(ygdn, axis=-1), wo_ref[sgu_w:, :], preferred_element_type=F32))
    x1 = x + g1_ref[0] * y
    ms = jnp.mean(x1 * x1, axis=-1, keepdims=True)
    gain = n2_ref[...] * (1.0 + sc_ref[0])
    h = ((x1 * lax.rsqrt(ms + RMS_EPS)) * gain + sh_ref[0]).astype(BF16)
    d_ff = w1_ref.shape[1]
    acc = None
    for j in range(d_ff // ff_tile):
        cs = slice(j * ff_tile, (j + 1) * ff_tile)
        a = jnp.dot(h, w1_ref[:, cs], preferred_element_type=F32)
        a = jnp.square(jnp.maximum(a, 0.0))
        part = jnp.dot(a.astype(BF16), w2_ref[cs, :], preferred_element_type=F32)
        acc = part if acc is None else acc + part
    x2 = x1 + g2_ref[0] * acc
    if final:
        ms2 = jnp.mean(x2 * x2, axis=-1, keepdims=True)
        x2 = x2 * lax.rsqrt(ms2 + RMS_EPS) * fg_ref[...]
    o_ref[0] = x2


def _mlp_call(x, ysgu, og, z, ng, mod, layer, n2, w_out, w1, w2, final_g, *, tm, ff_tile, final):
    bsz, t, d = x.shape
    mix_w = w_out.shape[0]
    d_ff = w1.shape[1]
    kern = functools.partial(_mlp_kernel, ff_tile=ff_tile, final=final)
    row_spec = lambda width: pl.BlockSpec((1, tm, width), lambda b, i: (b, i, 0))
    mod_spec = lambda k: pl.BlockSpec((None, 1, None, 1, d), lambda b, i: (layer, b, k, 0, 0))
    const2 = lambda shape: pl.BlockSpec(shape, lambda b, i: (0,) * len(shape),
                                        pipeline_mode=pl.Buffered(1))
    return pl.pallas_call(
        kern,
        grid=(bsz, t // tm),
        in_specs=[
            row_spec(d), row_spec(ysgu.shape[-1]), row_spec(og.shape[-1]), row_spec(z.shape[-1]),
            const2((1, GDN_DV)),
            mod_spec(MOD_GATE1), mod_spec(MOD_SHIFT2), mod_spec(MOD_SCALE2), mod_spec(MOD_GATE2), const2((1, d)),
            const2((mix_w, d)), const2((d, d_ff)), const2((d_ff, d)), const2((1, d)),
        ],
        out_specs=row_spec(d),
        out_shape=jax.ShapeDtypeStruct((bsz, t, d), F32),
        compiler_params=pltpu.CompilerParams(
            dimension_semantics=("arbitrary", "arbitrary"), vmem_limit_bytes=VMEM_LIMIT),
        name="outproj_mlp",
    )(x, ysgu, og, z, ng, mod, mod, mod, mod, n2, w_out, w1, w2, final_g)


def _pick_tile(t, want):
    tile = min(t, want)
    assert t % tile == 0
    return tile


def kernel(x, c, w_ada, b_ada, norm1_g, w_in, sgu_ln_g, sgu_ln_b, sgu_w, sgu_b, conv_w, a_log, dt_bias,
           gdn_norm_g, w_out, norm2_g, w_ff1, w_ff2, final_g):
    depth = w_ada.shape[0]
    bsz, t, d = x.shape
    sgu_width = sgu_ln_g.shape[-1]
    v_w = GDN_HEADS * GDN_DV
    qk_w = GDN_HEADS * GDN_DK
    in_w = w_in.shape[-1]
    gate_off = 2 * sgu_width + 2 * qk_w + 2 * v_w
    assert in_w == gate_off + 2 * GDN_HEADS
    assert t % SGU_CHUNK == 0 and t % GDN_BLOCK == 0

    tm = _pick_tile(t, 512)
    tg = _pick_tile(t, 512)

    mod = _ada_call(c, w_ada, b_ada)
    mod = mod.reshape(depth, bsz, N_MOD, 1, d)

    w_in_p = jnp.pad(w_in, ((0, 0), (0, 0), (0, LANES - 2 * GDN_HEADS))).astype(BF16)
    w_out_b = w_out.astype(BF16)
    w1_b = w_ff1.astype(BF16)
    w2_b = w_ff2.astype(BF16)
    sgu_b_t = jnp.pad(jnp.swapaxes(sgu_b, 1, 2), ((0, 0), (0, 0), (0, LANES - SGU_HEADS)))
    lane_pad = (GDN_HEADS, LANES - 2 * GDN_HEADS)
    alog_p = jnp.pad(a_log, ((0, 0), lane_pad)).reshape(depth, 1, LANES)
    dtb_p = jnp.pad(dt_bias, ((0, 0), lane_pad)).reshape(depth, 1, LANES)

    for l in range(depth):
        ysgu, z, og = _mix_call(
            x, mod, l, norm1_g[l].reshape(1, d), w_in_p[l],
            sgu_ln_g[l].reshape(1, -1), sgu_ln_b[l].reshape(1, -1), sgu_w[l], sgu_b_t[l],
            conv_w[l], alog_p[l], dtb_p[l],
            tm=tm, tg=tg, c=GDN_BLOCK, sgu_w=sgu_width, qk_w=qk_w, v_w=v_w)
        x = _mlp_call(x, ysgu, og, z, gdn_norm_g[l].reshape(1, -1), mod, l,
                      norm2_g[l].reshape(1, d), w_out_b[l], w1_b[l], w2_b[l], final_g.reshape(1, d),
                      tm=tm, ff_tile=1024, final=(l == depth - 1))
    return x
```

```python
import functools

import jax
import jax.numpy as jnp
from jax import lax
from jax.experimental import pallas as pl
from jax.experimental.pallas import tpu as pltpu

F32 = jnp.float32
BF16 = jnp.bfloat16

RMS_EPS = 1e-6
LN_EPS = 1e-5
N_MOD = 6
MOD_SHIFT1, MOD_SCALE1, MOD_GATE1, MOD_SHIFT2, MOD_SCALE2, MOD_GATE2 = range(N_MOD)
SGU_HEADS = 4
SGU_CHUNK = 128
GDN_HEADS = 4
GDN_DK = 128
GDN_DV = 128
CONV_WIDTH = 4
GDN_BLOCK = 128

MERGE_LEAD = 1500
ROW_CHUNK = 128
COL_CHUNK = 256

LANES = 128
SUBLANES = 8
BF16_ROWS = 16
VMEM_LIMIT = 56 * 1024 * 1024


def _bdot(a, b):
    return jnp.dot(a.astype(BF16), b.astype(BF16), preferred_element_type=F32)


def _bdot_nt(a, b):
    return lax.dot_general(a.astype(BF16), b.astype(BF16), (((1,), (1,)), ((), ())),
                           preferred_element_type=F32)


def _split3(a):
    hi = a.astype(BF16)
    r1 = a - hi.astype(F32)
    mid = r1.astype(BF16)
    lo = (r1 - mid.astype(F32)).astype(BF16)
    return hi, mid, lo


_LOG2E = 1.4426950408889634
_GELU_C0 = 0.7978845608028654
_GELU_C1 = 0.044715 * _GELU_C0


def _silu(a):
    return a / (1.0 + jnp.exp2(a * (-_LOG2E)))


def _gelu_tanh(a):
    t = a * a
    e = jnp.exp2(a * ((-2.0 * _LOG2E * _GELU_C0) + (-2.0 * _LOG2E * _GELU_C1) * t))
    return a / (1.0 + e)


def _ada_kernel(c_ref, w_ref, b_ref, o_ref):
    o_ref[0] = _bdot(_silu(c_ref[...]), w_ref[0]) + b_ref[0]


def _ada_call(c, w_ada, b_ada):
    depth, d, n = w_ada.shape
    bsz = c.shape[0]
    tn = 1536
    assert n % tn == 0
    return pl.pallas_call(
        _ada_kernel,
        grid=(depth, n // tn),
        in_specs=[
            pl.BlockSpec((bsz, d), lambda l, j: (0, 0)),
            pl.BlockSpec((1, d, tn), lambda l, j: (l, 0, j)),
            pl.BlockSpec((1, 1, tn), lambda l, j: (l, 0, j)),
        ],
        out_specs=pl.BlockSpec((1, bsz, tn), lambda l, j: (l, 0, j)),
        out_shape=jax.ShapeDtypeStruct((depth, bsz, n), F32),
        compiler_params=pltpu.CompilerParams(
            dimension_semantics=("arbitrary", "arbitrary"), vmem_limit_bytes=VMEM_LIMIT),
        name="ada_mod",
    )(c, w_ada, b_ada.reshape(depth, 1, n))


def _inproj_phases(x_ref, sh_ref, sc_ref, g_ref, w_ref, lng_ref, lnb_ref, sw_ref, sb_ref, cw_ref, alog_ref, dtb_ref,
                   ysgu_ref, z_ref, xe_ref, qkv_s, gates_s, seq_start, *, sgu_w, qk_w, v_w):
    tm = x_ref.shape[1]
    rows = [slice(r0, r0 + ROW_CHUNK) for r0 in range(0, tm, ROW_CHUNK)]
    o0 = 2 * sgu_w
    conv_c = 2 * qk_w + v_w
    o1 = o0 + conv_c

    h_parts = []
    gain = g_ref[...] * (1.0 + sc_ref[0])
    for rs in rows:
        yield 0, 250
        x = x_ref[0, rs, :]
        ms = jnp.mean(x * x, axis=-1, keepdims=True)
        h_parts.append(((x * lax.rsqrt(ms + RMS_EPS)) * gain + sh_ref[0]).astype(BF16))
    h = jnp.concatenate(h_parts, axis=0)

    def proj(c0, c1):
        parts = []
        for p0 in range(c0, c1, COL_CHUNK):
            yield 512, 0
            parts.append(jnp.dot(h, w_ref[:, p0:min(p0 + COL_CHUNK, c1)], preferred_element_type=F32))
        return parts[0] if len(parts) == 1 else jnp.concatenate(parts, axis=-1)

    p_u = yield from proj(0, sgu_w)
    p_v = yield from proj(sgu_w, o0)
    u_parts, vln_parts = [], []
    for rs in rows:
        yield 0, 200
        u_parts.append(_gelu_tanh(p_u[rs]))
        yield 0, 320
        vs = _gelu_tanh(p_v[rs])
        mu = jnp.mean(vs, axis=-1, keepdims=True)
        vc = vs - mu
        var = jnp.mean(vc * vc, axis=-1, keepdims=True)
        vln_parts.append((vc * lax.rsqrt(var + LN_EPS) * lng_ref[...] + lnb_ref[...]).astype(BF16))
    hd = sgu_w // SGU_HEADS
    row = lax.broadcasted_iota(jnp.int32, (SGU_CHUNK, SGU_CHUNK), 0)
    col = lax.broadcasted_iota(jnp.int32, (SGU_CHUNK, SGU_CHUNK), 1)
    causal = row >= col
    sb = sb_ref[...]
    assert ROW_CHUNK == SGU_CHUNK
    for hh in range(SGU_HEADS):
        yield 130, 70
        w_h = jnp.where(causal, sw_ref[hh], 0.0).astype(BF16)
        b_h = sb[:, hh:hh + 1]
        cs = slice(hh * hd, (hh + 1) * hd)
        for rs, u_c, vln_c in zip(rows, u_parts, vln_parts):
            mixed = jnp.dot(w_h, vln_c[:, cs], preferred_element_type=F32) + b_h
            ysgu_ref[0, rs, cs] = (u_c[:, cs] * mixed).astype(ysgu_ref.dtype)

    xe_ref[0:SUBLANES, :] = jnp.where(seq_start, 0.0, xe_ref[tm:tm + SUBLANES, :])
    cw = cw_ref[...]
    for part, width in enumerate((qk_w, qk_w, v_w)):
        c0 = part * qk_w
        cs_all = slice(c0, c0 + width)
        xe_ref[SUBLANES:SUBLANES + tm, cs_all] = yield from proj(o0 + c0, o0 + c0 + width)
        for rs in rows:
            base = SUBLANES + rs.start
            for hh in range(GDN_HEADS):
                yield 0, (100 if part == 2 else 125)
                cs = slice(c0 + hh * GDN_DK, c0 + (hh + 1) * GDN_DK)
                acc = xe_ref[base:base + ROW_CHUNK, cs] * cw[CONV_WIDTH - 1:CONV_WIDTH, cs]
                for j in range(CONV_WIDTH - 1):
                    off = base - (CONV_WIDTH - 1) + j
                    acc = acc + xe_ref[off:off + ROW_CHUNK, cs] * cw[j:j + 1, cs]
                act = _silu(acc)
                if part < 2:
                    inv = lax.rsqrt(jnp.sum(act * act, axis=-1, keepdims=True) + RMS_EPS)
                    act = act * (inv * (GDN_DK ** -0.5) if part == 0 else inv)
                qkv_s[rs, cs] = act.astype(qkv_s.dtype)

    p_zg = yield from proj(o1, w_ref.shape[1])
    z_ref[0] = p_zg[:, :v_w]

    yield 0, 250
    ba = p_zg[:, v_w:]
    lane = lax.broadcasted_iota(jnp.int32, ba.shape, 1)
    sp_in = ba + dtb_ref[...]
    softplus = jnp.maximum(sp_in, 0.0) + jnp.log1p(jnp.exp(-jnp.abs(sp_in)))
    g_all = -jnp.exp(alog_ref[...]) * softplus
    gates_s[...] = jnp.where(lane < GDN_HEADS, jax.nn.sigmoid(ba),
                             jnp.where(lane < 2 * GDN_HEADS, g_all, 0.0))


def _gdn_phases(qkv_s, gates_s, og_ref, s_ref, seq_start, *, tm, tg, c):
    nh = GDN_HEADS
    qk_w = nh * GDN_DK
    yield 0, 100
    gates_all = gates_s[...]
    staged = []
    for blk in range(tm // c):
        ra = slice(blk * c, (blk + 1) * c)
        staged.append([(qkv_s[ra, hh * GDN_DK:(hh + 1) * GDN_DK],
                        qkv_s[ra, qk_w + hh * GDN_DK:qk_w + (hh + 1) * GDN_DK],
                        qkv_s[ra, 2 * qk_w + hh * GDN_DV:2 * qk_w + (hh + 1) * GDN_DV]) for hh in range(nh)])
    st = [jnp.where(seq_start, 0.0, s_ref[hh]) for hh in range(nh)]

    r = lax.broadcasted_iota(jnp.int32, (tg, tg), 0)
    s = lax.broadcasted_iota(jnp.int32, (tg, tg), 1)
    tri = jnp.where((r >= s) & ((r // c) == (s // c)), 1.0, 0.0).astype(BF16)
    rc = lax.broadcasted_iota(jnp.int32, (c, c), 0)
    sc = lax.broadcasted_iota(jnp.int32, (c, c), 1)
    incl = rc >= sc
    eye_b = jnp.where(rc == sc, 1.0, 0.0).astype(BF16)
    n_lvl = c.bit_length() - 1
    lvl_sel = [jnp.where(((rc >> (j + 1)) == (sc >> (j + 1))) & (((rc >> j) & 1) == 1) & (((sc >> j) & 1) == 0),
                         1.0, 0.0).astype(BF16) for j in range(n_lvl)]
    d = functools.partial(jnp.dot, preferred_element_type=F32)

    for grp in range(tm // tg):
        g0 = grp * tg
        ni = nh * (tg // c)
        yield 3 * tg * tg // 1024, tg
        gates = gates_all[g0:g0 + tg, :]
        g_hi, g_mid, g_lo = _split3(gates)
        gc = d(tri, g_hi) + (d(tri, g_mid) + d(tri, g_lo))
        gc_t = gc.T

        items = [(b, hh) for b in range(tg // c) for hh in range(nh)]
        prep = []
        for b, hh in items:
            if hh == 0:
                yield 0, 600
            rs = slice(b * c, (b + 1) * c)
            qh, kh, vh = staged[g0 // c + b][hh]
            kf = kh.astype(F32)
            beta = gates[rs, hh:hh + 1]
            gcol = gc[rs, nh + hh:nh + hh + 1]
            grow = gc_t[nh + hh:nh + hh + 1, rs]
            g_last = gcol[c - 1:c, :]
            decay = jnp.where(incl, jnp.exp(jnp.where(incl, gcol - grow, 0.0)), 0.0)
            gamma = jnp.exp(gcol)
            prep.append(dict(
                kq_lhs=jnp.concatenate([kh, qh], axis=0), k=kh, beta=beta, decay=decay, g_last=g_last,
                rhs=jnp.concatenate([(beta * vh.astype(F32)).astype(BF16), ((beta * gamma) * kf).astype(BF16)],
                                    axis=-1),
                q_dec=(qh.astype(F32) * gamma).astype(BF16),
                kd_t=(kf * jnp.exp(g_last - gcol)).T.astype(BF16)))

        yield ni * c // 2, 0
        kq = [_bdot_nt(p["kq_lhs"], p["k"]) for p in prep]
        yield 0, ni * 25
        m_mats = [p["beta"] * a[:c] * p["decay"] for p, a in zip(prep, kq)]
        qks = [(a[c:] * p["decay"]).astype(BF16) for p, a in zip(prep, kq)]

        m_b = [m.astype(BF16) for m in m_mats]
        x_inv = [eye_b - m * lvl_sel[0] for m in m_b]
        for j in range(1, n_lvl):
            blk = 1 << j
            if blk >= BF16_ROWS:
                sel = [slice(r0, r0 + blk) for r0 in range(blk, c, 2 * blk)]
                rows_of = lambda a: jnp.concatenate([a[rs] for rs in sel], axis=0) if len(sel) > 1 else a[sel[0]]
                sel_mask = rows_of(lvl_sel[j])
                yield ni * c // 8, ni * 4
                y_j = [d(rows_of(m) * sel_mask, x).astype(BF16) for m, x in zip(m_b, x_inv)]
                zero = jnp.zeros((blk, c), BF16)
                y_full = [jnp.concatenate([piece for k in range(len(sel)) for piece in (zero, y[k * blk:(k + 1) * blk])],
                                          axis=0) for y in y_j]
                yield ni * c // 8, ni * 4
                x_new = []
                for x, y in zip(x_inv, y_full):
                    upd = rows_of(x) - d(rows_of(x), y).astype(BF16)
                    x_new.append(jnp.concatenate(
                        [piece for k, rs in enumerate(sel)
                         for piece in (x[rs.start - blk:rs.start], upd[k * blk:(k + 1) * blk])], axis=0))
                x_inv = x_new
            else:
                yield ni * c // 4, ni * 8
                y_j = [d(m * lvl_sel[j], x).astype(BF16) for m, x in zip(m_b, x_inv)]
                yield ni * c // 4, ni * 8
                x_inv = [x - d(x, y).astype(BF16) for x, y in zip(x_inv, y_j)]
        yield ni * c // 4, 0
        sols =[d(x, p["rhs"]) for x, p in zip(x_inv, prep)]

        for b in range(tg // c):
            yield nh * c, nh * 50
            ra = slice(g0 + b * c, g0 + (b + 1) * c)
            idx = [b * nh + hh for hh in range(nh)]
            ws = [_bdot(jnp.concatenate([sols[n][:, GDN_DV:].astype(BF16), prep[n]["q_dec"]], axis=0), st[hh])
                  for hh, n in enumerate(idx)]
            w_b = [(sols[n][:, :GDN_DV] - ws[hh][:c]).astype(BF16) for hh, n in enumerate(idx)]
            for hh, n in enumerate(idx):
                og_ref[0, ra, hh * GDN_DV:(hh + 1) * GDN_DV] = ws[hh][c:] + _bdot(qks[n], w_b[hh])
            st = [jnp.exp(prep[n]["g_last"]) * st[hh] + _bdot(prep[n]["kd_t"], w_b[hh])
                  for hh, n in enumerate(idx)]
    for hh in range(nh):
        s_ref[hh] = st[hh]


def _merge_phases(*stages):
    pending = {}
    for stage in stages:
        cost = next(stage, None)
        if cost is not None:
            pending[stage] = cost
    given = {stage: 0 for stage in pending}
    tot_m = tot_v = 0
    while pending:
        floor = min(given[g] for g in pending)
        near = [g for g in pending if given[g] <= floor + MERGE_LEAD]
        stage = min(near, key=lambda g: (abs((tot_m + pending[g][0]) - (tot_v + pending[g][1])), given[g]))
        tot_m, tot_v = tot_m + pending[stage][0], tot_v + pending[stage][1]
        given[stage] += sum(pending.pop(stage))
        cost = next(stage, None)
        if cost is not None:
            pending[stage] = cost


def _mix_kernel(x_ref, sh_ref, sc_ref, g_ref, w_ref, lng_ref, lnb_ref, sw_ref, sb_ref, cw_ref, alog_ref, dtb_ref,
                ysgu_ref, z_ref, og_ref, xe_ref, qkv_s, gates_s, s_ref,
                *, tm, tg, c, tiles_per_seq, n_tiles, sgu_w, qk_w, v_w):
    n = pl.program_id(0)

    @pl.when(n == 0)
    def _():
        xe_ref[...] = jnp.zeros(xe_ref.shape, xe_ref.dtype)
        qkv_s[...] = jnp.zeros(qkv_s.shape, qkv_s.dtype)
        gates_s[...] = jnp.zeros(gates_s.shape, gates_s.dtype)
        s_ref[...] = jnp.zeros(s_ref.shape, s_ref.dtype)

    gdn_start = jnp.maximum(n - 1, 0) % tiles_per_seq == 0
    in_start = jnp.minimum(n, n_tiles - 1) % tiles_per_seq == 0
    gdn = _gdn_phases(qkv_s, gates_s, og_ref, s_ref, gdn_start, tm=tm, tg=tg, c=c)
    inproj = _inproj_phases(x_ref, sh_ref, sc_ref, g_ref, w_ref, lng_ref, lnb_ref, sw_ref, sb_ref, cw_ref,
                            alog_ref, dtb_ref, ysgu_ref, z_ref, xe_ref, qkv_s, gates_s, in_start,
                            sgu_w=sgu_w, qk_w=qk_w, v_w=v_w)
    _merge_phases(gdn, inproj)


def _mix_call(x, mod, layer, g, w_in_p, ln_g, ln_b, sgu_w_l, sgu_b_t, conv_w, alog_p, dtb_p,
              *, tm, tg, c, sgu_w, qk_w, v_w):
    bsz, t, d = x.shape
    in_pad = w_in_p.shape[2]
    conv_c = 2 * qk_w + v_w
    tps = t // tm
    n_tiles = bsz * tps
    kern = functools.partial(_mix_kernel, tm=tm, tg=tg, c=c, tiles_per_seq=tps, n_tiles=n_tiles,
                             sgu_w=sgu_w, qk_w=qk_w, v_w=v_w)

    def in_idx(n):
        k = jnp.minimum(n, n_tiles - 1)
        return k // tps, k % tps

    def gdn_idx(n):
        k = jnp.maximum(n - 1, 0)
        return k // tps, k % tps

    in_row = lambda width: pl.BlockSpec((1, tm, width), lambda n: (*in_idx(n), 0))
    mod_spec = lambda k: pl.BlockSpec((None, 1, None, 1, d), lambda n: (layer, in_idx(n)[0], k, 0, 0))
    per_layer = lambda shape, **kw: pl.BlockSpec((None,) + shape, lambda n: (layer,) + (0,) * len(shape), **kw)
    return pl.pallas_call(
        kern,
        grid=(n_tiles + 1,),
        in_specs=[
            in_row(d), mod_spec(MOD_SHIFT1), mod_spec(MOD_SCALE1), per_layer((1, d)),
            per_layer((d, in_pad), pipeline_mode=pl.Buffered(1)),
            per_layer((1, sgu_w)), per_layer((1, sgu_w)),
            per_layer((SGU_HEADS, SGU_CHUNK, SGU_CHUNK)), per_layer((SGU_CHUNK, LANES)),
            per_layer((CONV_WIDTH, conv_c)), per_layer((1, LANES)), per_layer((1, LANES)),
        ],
        out_specs=[in_row(sgu_w), in_row(v_w),
                   pl.BlockSpec((1, tm, v_w), lambda n: (*gdn_idx(n), 0))],
        out_shape=[
            jax.ShapeDtypeStruct((bsz, t, sgu_w), BF16),
            jax.ShapeDtypeStruct((bsz, t, v_w), F32),
            jax.ShapeDtypeStruct((bsz, t, v_w), F32),
        ],
        scratch_shapes=[
            pltpu.VMEM((tm + SUBLANES, conv_c), F32),
            pltpu.VMEM((tm, conv_c), BF16),
            pltpu.VMEM((tm, LANES), F32),
            pltpu.VMEM((GDN_HEADS, GDN_DK, GDN_DV), F32),
        ],
        compiler_params=pltpu.CompilerParams(
            dimension_semantics=("arbitrary",), vmem_limit_bytes=VMEM_LIMIT),
        name="token_mix",
    )(x, mod, mod, g, w_in_p, ln_g, ln_b, sgu_w_l, sgu_b_t, conv_w, alog_p, dtb_p)


def _mlp_kernel(x_ref, ysgu_ref, og_ref, z_ref, ng_ref, g1_ref, sh_ref, sc_ref, g2_ref, n2_ref, wo_ref,
                w1_ref, w2_ref, fg_ref, o_ref, *, ff_tile, final):
    x = x_ref[0]
    sgu_w = ysgu_ref.shape[-1]
    ng = ng_ref[...]
    ygdn = []
    for hh in range(GDN_HEADS):
        cs = slice(hh * GDN_DV, (hh + 1) * GDN_DV)
        oh = og_ref[0, :, cs]
        oh = oh * lax.rsqrt(jnp.mean(oh * oh, axis=-1, keepdims=True) + RMS_EPS) * ng
        ygdn.append((oh * _silu(z_ref[0, :, cs])).astype(BF16))
    y = (jnp.dot(ysgu_ref[0], wo_ref[:sgu_w, :], preferred_element_type=F32)
         + jnp.dot(jnp.concatenate(ygdn, axis=-1), wo_ref[sgu_w:, :], preferred_element_type=F32))
    x1 = x + g1_ref[0] * y
    ms = jnp.mean(x1 * x1, axis=-1, keepdims=True)
    gain = n2_ref[...] * (1.0 + sc_ref[0])
    h = ((x1 * lax.rsqrt(ms + RMS_EPS)) * gain + sh_ref[0]).astype(BF16)
    d_ff = w1_ref.shape[1]
    acc = None
    for j in range(d_ff // ff_tile):
        cs = slice(j * ff_tile, (j + 1) * ff_tile)
        a = jnp.dot(h, w1_ref[:, cs], preferred_element_type=F32)
        a = jnp.square(jnp.maximum(a, 0.0))
        part = jnp.dot(a.astype(BF16), w2_ref[cs, :], preferred_element_type=F32)
        acc = part if acc is None else acc + part
    x2 = x1 + g2_ref[0] * acc
    if final:
        ms2 = jnp.mean(x2 * x2, axis=-1, keepdims=True)
        x2 = x2 * lax.rsqrt(ms2 + RMS_EPS) * fg_ref[...]
    o_ref[0] = x2


def _mlp_call(x, ysgu, og, z, ng, mod, layer, n2, w_out, w1, w2, final_g, *, tm, ff_tile, final):
    bsz, t, d = x.shape
    mix_w = w_out.shape[1]
    d_ff = w1.shape[2]
    kern = functools.partial(_mlp_kernel, ff_tile=ff_tile, final=final)
    row_spec = lambda width: pl.BlockSpec((1, tm, width), lambda b, i: (b, i, 0))
    mod_spec = lambda k: pl.BlockSpec((None, 1, None, 1, d), lambda b, i: (layer, b, k, 0, 0))
    per_layer = lambda shape: pl.BlockSpec((None,) + shape, lambda b, i: (layer,) + (0,) * len(shape),
                                           pipeline_mode=pl.Buffered(1))
    return pl.pallas_call(
        kern,
        grid=(bsz, t // tm),
        in_specs=[
            row_spec(d), row_spec(ysgu.shape[-1]), row_spec(og.shape[-1]), row_spec(z.shape[-1]),
            per_layer((1, GDN_DV)),
            mod_spec(MOD_GATE1), mod_spec(MOD_SHIFT2), mod_spec(MOD_SCALE2), mod_spec(MOD_GATE2),
            per_layer((1, d)),
            per_layer((mix_w, d)), per_layer((d, d_ff)), per_layer((d_ff, d)),
            pl.BlockSpec((1, d), lambda b, i: (0, 0)),
        ],
        out_specs=row_spec(d),
        out_shape=jax.ShapeDtypeStruct((bsz, t, d), F32),
        compiler_params=pltpu.CompilerParams(
            dimension_semantics=("arbitrary", "arbitrary"), vmem_limit_bytes=VMEM_LIMIT),
        name="outproj_mlp",
    )(x, ysgu, og, z, ng, mod, mod, mod, mod, n2, w_out, w1, w2, final_g)


def _pick_tile(t, want):
    tile = min(t, want)
    assert t % tile == 0
    return tile


def kernel(x, c, w_ada, b_ada, norm1_g, w_in, sgu_ln_g, sgu_ln_b, sgu_w, sgu_b, conv_w, a_log, dt_bias,
           gdn_norm_g, w_out, norm2_g, w_ff1, w_ff2, final_g):
    depth = w_ada.shape[0]
    bsz, t, d = x.shape
    sgu_width = sgu_ln_g.shape[-1]
    v_w = GDN_HEADS * GDN_DV
    qk_w = GDN_HEADS * GDN_DK
    in_w = w_in.shape[-1]
    gate_off = 2 * sgu_width + 2 * qk_w + 2 * v_w
    assert in_w == gate_off + 2 * GDN_HEADS
    assert t % SGU_CHUNK == 0 and t % GDN_BLOCK == 0

    tm = _pick_tile(t, 512)
    tg = _pick_tile(t, 512)

    mod = _ada_call(c, w_ada, b_ada)
    mod = mod.reshape(depth, bsz, N_MOD, 1, d)

    w_in_p = jnp.pad(w_in, ((0, 0), (0, 0), (0, LANES - 2 * GDN_HEADS))).astype(BF16)
    w_out_b = w_out.astype(BF16)
    w1_b = w_ff1.astype(BF16)
    w2_b = w_ff2.astype(BF16)
    sgu_b_t = jnp.pad(jnp.swapaxes(sgu_b, 1, 2), ((0, 0), (0, 0), (0, LANES - SGU_HEADS)))
    lane_pad = (GDN_HEADS, LANES - 2 * GDN_HEADS)
    alog_p = jnp.pad(a_log, ((0, 0), lane_pad)).reshape(depth, 1, LANES)
    dtb_p = jnp.pad(dt_bias, ((0, 0), lane_pad)).reshape(depth, 1, LANES)

    row = lambda a: a.reshape(depth, 1, a.shape[-1])
    for l in range(depth):
        ysgu, z, og = _mix_call(
            x, mod, l, row(norm1_g), w_in_p, row(sgu_ln_g), row(sgu_ln_b), sgu_w, sgu_b_t,
            conv_w, alog_p, dtb_p,
            tm=tm, tg=tg, c=GDN_BLOCK, sgu_w=sgu_width, qk_w=qk_w, v_w=v_w)
        x = _mlp_call(x, ysgu, og, z, row(gdn_norm_g), mod, l, row(norm2_g), w_out_b, w1_b, w2_b,
                      final_g.reshape(1, d), tm=tm, ff_tile=1024, final=(l == depth - 1))
    return x
```
